```python
import functools
import jax, jax.numpy as jnp
from jax import lax
import numpy as np

D_MODEL = 2048
BATCH = 2
SEQ = 8192
DEPTH = 1
DEC_BATCH = 8
DEC_SEQ = 32
PAST_LEN = 2048

CHUNK = 64
BAND_CHUNKS = 8
BAND = BAND_CHUNKS * CHUNK
HEAD_DIM = 128
N_HEADS_A = 8
N_HEADS_B = 8
WIDTH_A = N_HEADS_A * HEAD_DIM
WIDTH_B = N_HEADS_B * HEAD_DIM
REL_CLIP = 128
SB_BLOCK = 128
D_FF = -(-8 * D_MODEL // (3 * 256)) * 256
IN_COLS = 3 * WIDTH_A + 3 * WIDTH_B + 2 * D_MODEL
EPS = 1e-6
ATTN_SCALE = HEAD_DIM ** -0.5

kernel_name = 'chunk_band_stickbreak_hybrid_step'


def rms_norm(x, g):
    xf = x.astype(jnp.float32)
    y = xf * lax.rsqrt(jnp.mean(xf * xf, axis=-1, keepdims=True) + EPS)
    return (y * g.astype(jnp.float32)).astype(x.dtype)


def ada_mod(c, w_ada, b_ada):
    m = jax.nn.silu(c) @ w_ada + b_ada
    return jnp.split(m[:, None, :], 6, axis=-1)


def split_heads(t, n_heads):
    return t.reshape(t.shape[:-1] + (n_heads, HEAD_DIM))


def mixer_proj(h, w_in):
    p = h @ w_in
    offs = [WIDTH_A, 2 * WIDTH_A, 3 * WIDTH_A, 3 * WIDTH_A + WIDTH_B,
            3 * WIDTH_A + 2 * WIDTH_B, 3 * WIDTH_A + 3 * WIDTH_B,
            3 * WIDTH_A + 3 * WIDTH_B + D_MODEL]
    qa, ka, va, qb, kb, vb, ga, gb = jnp.split(p, offs, axis=-1)
    return (split_heads(qa, N_HEADS_A), split_heads(ka, N_HEADS_A), split_heads(va, N_HEADS_A),
            split_heads(qb, N_HEADS_B), split_heads(kb, N_HEADS_B), split_heads(vb, N_HEADS_B),
            ga, gb)


def band_attn(q, k, v, q_pos, k_pos, rel_bias):
    s = jnp.einsum('bqhd,bkhd->bhqk', q, k, preferred_element_type=jnp.float32) * ATTN_SCALE
    rel = jnp.clip(q_pos[:, None] - k_pos[None, :], -REL_CLIP, REL_CLIP) + REL_CLIP
    s = s + rel_bias.astype(jnp.float32)[:, rel][None]
    s = jnp.where((k_pos >= 0)[None, None, None, :], s, -jnp.inf)
    p = jax.nn.softmax(s, axis=-1)
    return jnp.einsum('bhqk,bkhd->bqhd', p.astype(v.dtype), v)


def chunk_attn_prompt(q, k, v, rel_bias):
    b, s_len, h, d = q.shape
    kp = jnp.pad(k, ((0, 0), (BAND, 0), (0, 0), (0, 0)))
    vp = jnp.pad(v, ((0, 0), (BAND, 0), (0, 0), (0, 0)))

    def one_chunk(ci):
        start = ci * CHUNK
        qc = lax.dynamic_slice_in_dim(q, start, CHUNK, axis=1)
        kc = lax.dynamic_slice_in_dim(kp, start, BAND + CHUNK, axis=1)
        vc = lax.dynamic_slice_in_dim(vp, start, BAND + CHUNK, axis=1)
        q_pos = start + jnp.arange(CHUNK)
        k_pos = start - BAND + jnp.arange(BAND + CHUNK)
        return band_attn(qc, kc, vc, q_pos, k_pos, rel_bias)

    o = lax.map(one_chunk, jnp.arange(s_len // CHUNK))
    return jnp.moveaxis(o, 0, 1).reshape(b, s_len, h, d)


def chunk_attn_sample(q, k, v, cache_k, cache_v, rel_bias):
    n = q.shape[1]
    lc = cache_k.shape[1]
    kk = jnp.concatenate([cache_k.astype(k.dtype), k], axis=1)
    vv = jnp.concatenate([cache_v.astype(v.dtype), v], axis=1)
    q_pos = PAST_LEN + jnp.arange(n)
    k_pos = PAST_LEN - lc + jnp.arange(lc + n)
    return band_attn(q, kk, vv, q_pos, k_pos, rel_bias)


def stick_block(q, k, v, q_pos, k_pos):
    z = jnp.einsum('bqhd,bkhd->bhqk', q, k, preferred_element_type=jnp.float32) * ATTN_SCALE
    causal = (k_pos[None, :] < q_pos[:, None])[None, None]
    log_1m = jnp.where(causal, jax.nn.log_sigmoid(-z), 0.0)
    later = lax.cumsum(log_1m, axis=3, reverse=True) - log_1m
    a = jnp.where(causal, jnp.exp(jax.nn.log_sigmoid(z) + later), 0.0)
    return jnp.einsum('bhqk,bkhd->bqhd', a.astype(v.dtype), v)


def stick_prompt(q, k, v):
    b, s_len, h, d = q.shape
    k_pos = jnp.arange(s_len)

    def one_block(bi):
        start = bi * SB_BLOCK
        qb = lax.dynamic_slice_in_dim(q, start, SB_BLOCK, axis=1)
        return stick_block(qb, k, v, start + jnp.arange(SB_BLOCK), k_pos)

    o = lax.map(one_block, jnp.arange(s_len // SB_BLOCK))
    return jnp.moveaxis(o, 0, 1).reshape(b, s_len, h, d)


def stick_sample(q, k, v, cache_k, cache_v):
    n = q.shape[1]
    lc = cache_k.shape[1]
    kk = jnp.concatenate([cache_k.astype(k.dtype), k], axis=1)
    vv = jnp.concatenate([cache_v.astype(v.dtype), v], axis=1)
    return stick_block(q, kk, vv, PAST_LEN + jnp.arange(n), jnp.arange(lc + n))


def swiglu(h, w_gate_up, w_down):
    g, u = jnp.split(h @ w_gate_up, 2, axis=-1)
    return (jax.nn.silu(g) * u) @ w_down


def block(x, c, mix_a, mix_b, w_ada, b_ada, g_mix, w_in, w_a_out, w_b_out, w_o,
          g_ffn, w_gate_up, w_down):
    sh1, sc1, gt1, sh2, sc2, gt2 = ada_mod(c, w_ada, b_ada)
    h = rms_norm(x, g_mix) * (1.0 + sc1) + sh1
    qa, ka, va, qb, kb, vb, ga, gb = mixer_proj(h, w_in)
    oa = mix_a(qa, ka, va)
    ob = mix_b(qb, kb, vb)
    ya = oa.reshape(oa.shape[:2] + (WIDTH_A,)) @ w_a_out
    yb = ob.reshape(ob.shape[:2] + (WIDTH_B,)) @ w_b_out
    merged = (jax.nn.sigmoid(ga) * ya + jax.nn.sigmoid(gb) * yb) @ w_o
    x = x + gt1 * merged
    h = rms_norm(x, g_ffn) * (1.0 + sc2) + sh2
    x = x + gt2 * swiglu(h, w_gate_up, w_down)
    return x, ka, va, kb, vb


def setup_inputs(seed: int = 0) -> dict:
    key = jax.random.key(seed)
    ks = jax.random.split(key, 24)
    f32 = jnp.float32

    def nrm(k, shape, scale):
        return jax.random.normal(k, shape, f32) * scale

    la = min(BAND, PAST_LEN)
    return {
        'x_prompt': nrm(ks[0], (BATCH, SEQ, D_MODEL), 1.0),
        'x_sample': nrm(ks[1], (DEC_BATCH, DEC_SEQ, D_MODEL), 1.0),
        'c_prompt': nrm(ks[2], (BATCH, D_MODEL), 1.0),
        'c_sample': nrm(ks[3], (DEC_BATCH, D_MODEL), 1.0),
        'cache_a_k': nrm(ks[4], (DEPTH, DEC_BATCH, la, N_HEADS_A, HEAD_DIM), 1.0),
        'cache_a_v': nrm(ks[5], (DEPTH, DEC_BATCH, la, N_HEADS_A, HEAD_DIM), 1.0),
        'cache_b_k': nrm(ks[6], (DEPTH, DEC_BATCH, PAST_LEN, N_HEADS_B, HEAD_DIM), 1.0),
        'cache_b_v': nrm(ks[7], (DEPTH, DEC_BATCH, PAST_LEN, N_HEADS_B, HEAD_DIM), 1.0),
        'w_ada': nrm(ks[8], (DEPTH, D_MODEL, 6 * D_MODEL), 0.5 * D_MODEL ** -0.5),
        'b_ada': nrm(ks[9], (DEPTH, 6 * D_MODEL), 0.01),
        'g_mix': 1.0 + nrm(ks[10], (DEPTH, D_MODEL), 0.02),
        'w_in': nrm(ks[11], (DEPTH, D_MODEL, IN_COLS), D_MODEL ** -0.5),
        'rel_bias': nrm(ks[12], (DEPTH, N_HEADS_A, 2 * REL_CLIP + 1), 0.5),
        'w_a_out': nrm(ks[13], (DEPTH, WIDTH_A, D_MODEL), WIDTH_A ** -0.5),
        'w_b_out': nrm(ks[14], (DEPTH, WIDTH_B, D_MODEL), WIDTH_B ** -0.5),
        'w_o': nrm(ks[15], (DEPTH, D_MODEL, D_MODEL), D_MODEL ** -0.5),
        'g_ffn': 1.0 + nrm(ks[16], (DEPTH, D_MODEL), 0.02),
        'w_gate_up': nrm(ks[17], (DEPTH, D_MODEL, 2 * D_FF), D_MODEL ** -0.5),
        'w_down': nrm(ks[18], (DEPTH, D_FF, D_MODEL), D_FF ** -0.5),
        'g_final': 1.0 + nrm(ks[19], (D_MODEL,), 0.02),
    }


def reference(x_prompt, x_sample, c_prompt, c_sample, cache_a_k, cache_a_v, cache_b_k, cache_b_v,
              w_ada, b_ada, g_mix, w_in, rel_bias, w_a_out, w_b_out, w_o, g_ffn, w_gate_up,
              w_down, g_final):
    xp, xs = x_prompt, x_sample
    ak_p, av_p, bk_p, bv_p = [], [], [], []
    ak_s, av_s, bk_s, bv_s = [], [], [], []
    for l in range(DEPTH):
        lw = (w_ada[l], b_ada[l], g_mix[l], w_in[l], w_a_out[l], w_b_out[l], w_o[l],
              g_ffn[l], w_gate_up[l], w_down[l])
        xp, ka, va, kb, vb = block(
            xp, c_prompt, functools.partial(chunk_attn_prompt, rel_bias=rel_bias[l]),
            stick_prompt, *lw)
        keep = min(BAND, ka.shape[1])
        ak_p.append(ka[:, -keep:]); av_p.append(va[:, -keep:])
        bk_p.append(kb); bv_p.append(vb)
        xs, ka, va, kb, vb = block(
            xs, c_sample,
            functools.partial(chunk_attn_sample, cache_k=cache_a_k[l], cache_v=cache_a_v[l],
                              rel_bias=rel_bias[l]),
            functools.partial(stick_sample, cache_k=cache_b_k[l], cache_v=cache_b_v[l]),
            *lw)
        ak_s.append(ka); av_s.append(va); bk_s.append(kb); bv_s.append(vb)
    y_prompt = rms_norm(xp, g_final)
    y_sample = rms_norm(xs, g_final)
    return (y_prompt, y_sample,
            jnp.stack(ak_p), jnp.stack(av_p), jnp.stack(bk_p), jnp.stack(bv_p),
            jnp.stack(ak_s), jnp.stack(av_s), jnp.stack(bk_s), jnp.stack(bv_s))
```

```python
import functools

import jax
import jax.numpy as jnp
from jax import lax
from jax.experimental import pallas as pl
from jax.experimental.pallas import tpu as pltpu

F32 = jnp.float32
BF16 = jnp.bfloat16

HEAD_DIM = 128
CHUNK = 64
BAND_CHUNKS = 8
BAND = BAND_CHUNKS * CHUNK
REL_CLIP = 128
EPS = 1e-6
ATTN_SCALE = HEAD_DIM ** -0.5
MASK_VALUE = -1e30

V7X_VMEM_BYTES = 64 * 1024 * 1024
V7X_LANES = 128
V7X_BF16_SUBLANES = 16

EXP_ZERO_BELOW = -104.0

NORM_ROWS = 16


def _vmem_limit(estimate_bytes):
    return int(min(V7X_VMEM_BYTES - (4 << 20), max(estimate_bytes * 5 // 4, 16 << 20)))


def _dot(a, b):
    return jnp.dot(a, b, preferred_element_type=F32)


def _dot_nt(a, b):
    return lax.dot_general(a, b, (((1,), (1,)), ((), ())), preferred_element_type=F32)


def _sigmoid(x):
    return 1.0 / (1.0 + jnp.exp(-x))


def _ada_kernel(c_ref, w_ref, b_ref, o_ref):
    c = c_ref[...]
    a = (c * _sigmoid(c)).astype(BF16)
    o_ref[...] = _dot(a, w_ref[...].astype(BF16)) + b_ref[...]


def _ada(c_all, w_ada, b_ada):
    rows, d = c_all.shape
    n = w_ada.shape[1]
    tn = 1024
    est = 2 * d * tn * 4 + d * tn * 2 + 4 * rows * (d + tn) * 4
    return pl.pallas_call(
        _ada_kernel,
        grid=(n // tn,),
        in_specs=[
            pl.BlockSpec((rows, d), lambda j: (0, 0)),
            pl.BlockSpec((d, tn), lambda j: (0, j)),
            pl.BlockSpec((1, tn), lambda j: (0, j)),
        ],
        out_specs=pl.BlockSpec((rows, tn), lambda j: (0, j)),
        out_shape=jax.ShapeDtypeStruct((rows, n), F32),
        compiler_params=pltpu.CompilerParams(
            dimension_semantics=("parallel",), vmem_limit_bytes=_vmem_limit(est)),
        name="ada",
    )(c_all, w_ada, b_ada.reshape(1, n))


def _norm_mod_rows(x_ref, g_ref, sc_ref, sh_ref, h_ref, groups, ts):
    gvec = g_ref[...]
    for gi in range(groups):
        scale = gvec * (1.0 + sc_ref[gi])
        shift = sh_ref[gi]

        def body(r, carry, gi=gi, scale=scale, shift=shift):
            r0 = pl.multiple_of(r * NORM_ROWS, NORM_ROWS)
            x = x_ref[gi, pl.ds(r0, NORM_ROWS), :]
            ms = jnp.mean(x * x, axis=-1, keepdims=True)
            h = x * lax.rsqrt(ms + EPS) * scale + shift
            h_ref[pl.ds(pl.multiple_of(gi * ts + r0, NORM_ROWS), NORM_ROWS), :] = h.astype(BF16)
            return carry

        lax.fori_loop(0, ts // NORM_ROWS, body, 0)


def _row_tiling(batch, seq, tile_rows):
    if seq >= tile_rows:
        assert seq % tile_rows == 0
        groups, ts, ns = 1, tile_rows, seq // tile_rows
    else:
        assert tile_rows % seq == 0 and batch % (tile_rows // seq) == 0
        groups, ts, ns = tile_rows // seq, seq, 1
    n_tiles = batch * seq // (groups * ts)
    if groups == 1:
        x_map = lambda i: (i // ns, i % ns, 0)
        b_map = lambda i: (i // ns, 0, 0)
    else:
        x_map = lambda i: (i, 0, 0)
        b_map = lambda i: (i, 0, 0)
    return groups, ts, n_tiles, x_map, b_map


N_QKV = 6


def _inproj_kernel(x_ref, g_ref, sc_ref, sh_ref, w_ref,
                   qa_ref, ka_ref, va_ref, qb_ref, kb_ref, vb_ref, gates_ref, h_ref, *, groups, ts):
    j = pl.program_id(1)

    @pl.when(j == 0)
    def _():
        _norm_mod_rows(x_ref, g_ref, sc_ref, sh_ref, h_ref, groups, ts)

    for idx, o_ref in enumerate((qa_ref, ka_ref, va_ref, qb_ref, kb_ref, vb_ref)):
        @pl.when(j == idx)
        def _(o_ref=o_ref):
            o_ref[...] = _dot(h_ref[...], w_ref[...]).astype(o_ref.dtype)

    @pl.when(j >= N_QKV)
    def _():
        gates_ref[...] = _dot(h_ref[...], w_ref[...]).astype(gates_ref.dtype)


def _inproj(x, g, sc, sh, w_in, tile_rows):
    batch, seq, d = x.shape
    rows = batch * seq
    width = 1024
    n_cols = w_in.shape[1]
    nj = n_cols // width
    n_gate_tiles = nj - N_QKV
    groups, ts, n_tiles, x_map, b_map = _row_tiling(batch, seq, tile_rows)
    tm = groups * ts
    xm = lambda i, j: x_map(i)
    bm = lambda i, j: b_map(i)
    row_spec = pl.BlockSpec((tm, width), lambda i, j: (i, 0))
    est = (2 * tm * d * 4 + 2 * d * width * 2 + tm * d * 2
           + 2 * tm * width * (2 + 4 + 4 + 2 + 4 + 4 + 4))
    out_dtypes = (BF16, F32, F32, BF16, F32, F32)
    return pl.pallas_call(
        functools.partial(_inproj_kernel, groups=groups, ts=ts),
        grid=(n_tiles, nj),
        in_specs=[
            pl.BlockSpec((groups, ts, d), xm),
            pl.BlockSpec((1, d), lambda i, j: (0, 0)),
            pl.BlockSpec((groups, 1, d), bm),
            pl.BlockSpec((groups, 1, d), bm),
            pl.BlockSpec((d, width), lambda i, j: (0, j)),
        ],
        out_specs=[row_spec] * N_QKV + [
            pl.BlockSpec((tm, width), lambda i, j: (i, jnp.maximum(j - N_QKV, 0)))],
        out_shape=[jax.ShapeDtypeStruct((rows, width), dt) for dt in out_dtypes]
        + [jax.ShapeDtypeStruct((rows, n_gate_tiles * width), F32)],
        scratch_shapes=[pltpu.VMEM((tm, d), BF16)],
        compiler_params=pltpu.CompilerParams(
            dimension_semantics=("parallel", "arbitrary"), vmem_limit_bytes=_vmem_limit(est)),
        name="inproj",
    )(x, g.reshape(1, d), sc, sh, w_in)


A_BLOCK = 256


def _attn_a_prompt_kernel(q_ref, k0_ref, k1_ref, k2_ref, v0_ref, v1_ref, v2_ref, bias_ref, o_ref):
    qi = pl.program_id(2)
    q = q_ref[...]
    scores = []
    for jb, k_ref in enumerate((k0_ref, k1_ref, k2_ref)):
        s = _dot_nt(q, k_ref[...].astype(BF16)) * ATTN_SCALE
        s = s + bias_ref[0, :, jb * A_BLOCK:(jb + 1) * A_BLOCK]
        scores.append(jnp.where(qi + (jb - 2) >= 0, s, MASK_VALUE))
    m = functools.reduce(jnp.maximum, [jnp.max(s, axis=-1, keepdims=True) for s in scores])
    es = [jnp.exp(s - m) for s in scores]
    l = functools.reduce(jnp.add, [jnp.sum(e, axis=-1, keepdims=True) for e in es])
    acc = functools.reduce(jnp.add, [
        _dot(e.astype(BF16), v_ref[...].astype(BF16))
        for e, v_ref in zip(es, (v0_ref, v1_ref, v2_ref))])
    o_ref[...] = (acc / l).astype(o_ref.dtype)


def _band_bias_prompt(rel_bias):
    r = jnp.arange(A_BLOCK)[:, None]
    c = jnp.arange(3 * A_BLOCK)[None, :]
    dist = r + 2 * A_BLOCK - c
    qc = r // CHUNK
    kc = c // CHUNK - (2 * A_BLOCK) // CHUNK
    in_band = (kc <= qc) & (kc >= qc - BAND_CHUNKS)
    idx = jnp.clip(dist, -REL_CLIP, REL_CLIP) + REL_CLIP
    return jnp.where(in_band[None], rel_bias.astype(F32)[:, idx], MASK_VALUE)


def _attn_a_prompt(q, k, v, rel_bias, batch, seq):
    rows, width = q.shape
    heads = width // HEAD_DIM
    nq = seq // A_BLOCK
    assert BAND == 2 * A_BLOCK and seq % A_BLOCK == 0
    bias = _band_bias_prompt(rel_bias)
    blk = (A_BLOCK, HEAD_DIM)

    def kv_spec(jb):
        return pl.BlockSpec(blk, lambda b, h, i: (b * nq + jnp.maximum(i + (jb - 2), 0), h))

    est = 2 * (A_BLOCK * HEAD_DIM * (2 + 6 * 4 + 2) + A_BLOCK * 3 * A_BLOCK * 4) + 8 * A_BLOCK * 3 * A_BLOCK * 4
    return pl.pallas_call(
        _attn_a_prompt_kernel,
        grid=(batch, heads, nq),
        in_specs=[pl.BlockSpec(blk, lambda b, h, i: (b * nq + i, h)),
                  kv_spec(0), kv_spec(1), kv_spec(2), kv_spec(0), kv_spec(1), kv_spec(2),
                  pl.BlockSpec((1, A_BLOCK, 3 * A_BLOCK), lambda b, h, i: (h, 0, 0))],
        out_specs=pl.BlockSpec(blk, lambda b, h, i: (b * nq + i, h)),
        out_shape=jax.ShapeDtypeStruct((rows, width), BF16),
        compiler_params=pltpu.CompilerParams(
            dimension_semantics=("parallel", "parallel", "parallel"),
            vmem_limit_bytes=_vmem_limit(est)),
        name="attn_a_prompt",
    )(q, k, k, k, v, v, v, bias)


def _attn_a_sample_kernel(q_ref, kc_ref, vc_ref, kn_ref, vn_ref, bc_ref, bn_ref, o_ref):
    q = q_ref[...]
    sc = _dot_nt(q, kc_ref[0].astype(BF16)) * ATTN_SCALE + bc_ref[0]
    sn = _dot_nt(q, kn_ref[...].astype(BF16)) * ATTN_SCALE + bn_ref[0]
    m = jnp.maximum(jnp.max(sc, axis=-1, keepdims=True), jnp.max(sn, axis=-1, keepdims=True))
    ec = jnp.exp(sc - m)
    en = jnp.exp(sn - m)
    l = jnp.sum(ec, axis=-1, keepdims=True) + jnp.sum(en, axis=-1, keepdims=True)
    acc = _dot(ec.astype(BF16), vc_ref[0].astype(BF16)) + _dot(en.astype(BF16), vn_ref[...].astype(BF16))
    o_ref[...] = (acc / l).astype(o_ref.dtype)


def _attn_a_sample(q, k, v, cache_k, cache_v, rel_bias, batch, n, past_len):
    rows, width = q.shape
    heads = width // HEAD_DIM
    lc = cache_k.shape[1]
    assert past_len >= lc
    dist = jnp.arange(n)[:, None] + lc - jnp.arange(lc + n)[None, :]
    bias = rel_bias.astype(F32)[:, jnp.clip(dist, -REL_CLIP, REL_CLIP) + REL_CLIP]
    blk = (n, HEAD_DIM)
    cblk = (1, lc, HEAD_DIM)
    est = 2 * (2 * lc * HEAD_DIM * 4 + n * lc * 4) + 8 * n * lc * 4
    return pl.pallas_call(
        _attn_a_sample_kernel,
        grid=(batch, heads),
        in_specs=[pl.BlockSpec(blk, lambda b, h: (b, h)),
                  pl.BlockSpec(cblk, lambda b, h: (b, 0, h)),
                  pl.BlockSpec(cblk, lambda b, h: (b, 0, h)),
                  pl.BlockSpec(blk, lambda b, h: (b, h)),
                  pl.BlockSpec(blk, lambda b, h: (b, h)),
                  pl.BlockSpec((1, n, lc), lambda b, h: (h, 0, 0)),
                  pl.BlockSpec((1, n, n), lambda b, h: (h, 0, 0))],
        out_specs=pl.BlockSpec(blk, lambda b, h: (b, h)),
        out_shape=jax.ShapeDtypeStruct((rows, width), BF16),
        compiler_params=pltpu.CompilerParams(
            dimension_semantics=("parallel", "parallel"), vmem_limit_bytes=_vmem_limit(est)),
        name="attn_a_sample",
    )(q, cache_k, cache_v, k, v, bias[:, :, :lc], bias[:, :, lc:])


B_BLOCK = 256


def _later_matrix(n):
    j = lax.broadcasted_iota(jnp.int32, (n, n), 0)
    s = lax.broadcasted_iota(jnp.int32, (n, n), 1)
    return jnp.where(j > s, 1.0, 0.0).astype(BF16)


def _stick_step(q, k, v, c, acc, later_mat, causal):
    z = _dot_nt(q, k) * ATTN_SCALE
    log_1m = -(jnp.maximum(z, 0.0) + jnp.log(1.0 + jnp.exp(-jnp.abs(z))))
    if causal is not None:
        log_1m = jnp.where(causal, log_1m, 0.0)
    hi = log_1m.astype(BF16)
    lo = (log_1m - hi.astype(F32)).astype(BF16)
    later = _dot(hi, later_mat) + _dot(lo, later_mat)
    a = jnp.exp(z + log_1m + later + c)
    if causal is not None:
        a = jnp.where(causal, a, 0.0)
    acc = acc + _dot(a.astype(BF16), v)
    c = c + jnp.sum(log_1m, axis=-1, keepdims=True)
    return c, acc


def _stick_sweep_past(q, k_ref, v_ref, n_blocks, tk, c, acc, later_mat):
    def cond(carry):
        j, c, _ = carry
        return jnp.logical_and(j >= 0, jnp.max(c) > EXP_ZERO_BELOW)

    def body(carry):
        j, c, acc = carry
        rows = pl.ds(pl.multiple_of(j * tk, tk), tk)
        c, acc = _stick_step(q, k_ref[rows, :].astype(BF16), v_ref[rows, :].astype(BF16),
                             c, acc, later_mat, None)
        return j - 1, c, acc

    _, c, acc = lax.while_loop(cond, body, (n_blocks - 1, c, acc))
    return acc


def _strict_causal(n):
    t = lax.broadcasted_iota(jnp.int32, (n, n), 0)
    s = lax.broadcasted_iota(jnp.int32, (n, n), 1)
    return s < t


def _attn_b_prompt_kernel(q_ref, k_ref, v_ref, o_ref):
    qi = pl.program_id(2)
    q = q_ref[...]
    k_ref = k_ref.at[0]
    v_ref = v_ref.at[0]
    later_mat = _later_matrix(B_BLOCK)
    rows = pl.ds(pl.multiple_of(qi * B_BLOCK, B_BLOCK), B_BLOCK)
    c = jnp.zeros((B_BLOCK, 1), F32)
    acc = jnp.zeros((B_BLOCK, HEAD_DIM), F32)
    c, acc = _stick_step(q, k_ref[rows, :].astype(BF16), v_ref[rows, :].astype(BF16),
                         c, acc, later_mat, _strict_causal(B_BLOCK))
    acc = _stick_sweep_past(q, k_ref, v_ref, qi, B_BLOCK, c, acc, later_mat)
    o_ref[...] = acc.astype(o_ref.dtype)


def _attn_b_prompt(q, k, v, batch, seq):
    rows, width = q.shape
    heads = width // HEAD_DIM
    nq = seq // B_BLOCK
    k3 = k.reshape(batch, seq, width)
    v3 = v.reshape(batch, seq, width)
    blk = (B_BLOCK, HEAD_DIM)
    kv_spec = pl.BlockSpec((1, seq, HEAD_DIM), lambda b, h, i: (b, 0, h))
    est = 2 * 2 * seq * HEAD_DIM * 4 + 24 * B_BLOCK * B_BLOCK * 4
    return pl.pallas_call(
        _attn_b_prompt_kernel,
        grid=(batch, heads, nq),
        in_specs=[pl.BlockSpec(blk, lambda b, h, i: (b * nq + i, h)), kv_spec, kv_spec],
        out_specs=pl.BlockSpec(blk, lambda b, h, i: (b * nq + i, h)),
        out_shape=jax.ShapeDtypeStruct((rows, width), BF16),
        compiler_params=pltpu.CompilerParams(
            dimension_semantics=("parallel", "parallel", "parallel"),
            vmem_limit_bytes=_vmem_limit(est)),
        name="attn_b_prompt",
    )(q, k3, v3)


def _attn_b_sample_kernel(q_ref, kc_ref, vc_ref, kn_ref, vn_ref, o_ref, *, n, n_blocks):
    q = q_ref[...]
    c = jnp.zeros((n, 1), F32)
    acc = jnp.zeros((n, HEAD_DIM), F32)
    c, acc = _stick_step(q, kn_ref[...].astype(BF16), vn_ref[...].astype(BF16),
                         c, acc, _later_matrix(n), _strict_causal(n))
    acc = _stick_sweep_past(q, kc_ref.at[0], vc_ref.at[0], n_blocks, B_BLOCK, c, acc,
                            _later_matrix(B_BLOCK))
    o_ref[...] = acc.astype(o_ref.dtype)


def _attn_b_sample(q, k, v, cache_k, cache_v, batch, n):
    rows, width = q.shape
    heads = width // HEAD_DIM
    lc = cache_k.shape[1]
    assert lc % B_BLOCK == 0
    blk = (n, HEAD_DIM)
    cblk = (1, lc, HEAD_DIM)
    est = 2 * 2 * lc * HEAD_DIM * 4 + 24 * n * B_BLOCK * 4 + 4 * B_BLOCK * B_BLOCK
    return pl.pallas_call(
        functools.partial(_attn_b_sample_kernel, n=n, n_blocks=lc // B_BLOCK),
        grid=(batch, heads),
        in_specs=[pl.BlockSpec(blk, lambda b, h: (b, h)),
                  pl.BlockSpec(cblk, lambda b, h: (b, 0, h)),
                  pl.BlockSpec(cblk, lambda b, h: (b, 0, h)),
                  pl.BlockSpec(blk, lambda b, h: (b, h)),
                  pl.BlockSpec(blk, lambda b, h: (b, h))],
        out_specs=pl.BlockSpec(blk, lambda b, h: (b, h)),
        out_shape=jax.ShapeDtypeStruct((rows, width), BF16),
        compiler_params=pltpu.CompilerParams(
            dimension_semantics=("parallel", "parallel"), vmem_limit_bytes=_vmem_limit(est)),
        name="attn_b_sample",
    )(q, cache_k, cache_v, k, v)


MIX_COLS = 512


def _mix_out_kernel(x_ref, oa_ref, ob_ref, gates_ref, gt_ref, wa_ref, wb_ref, wo_ref, o_ref, y_ref,
                    *, groups, ts):
    d = o_ref.shape[-1]
    oa = oa_ref[...]
    ob = ob_ref[...]
    for c0 in range(0, d, MIX_COLS):
        cols = slice(c0, c0 + MIX_COLS)
        ya = _dot(oa, wa_ref[:, cols])
        yb = _dot(ob, wb_ref[:, cols])
        ga = gates_ref[:, cols]
        gb = gates_ref[:, d + c0:d + c0 + MIX_COLS]
        y_ref[:, cols] = (_sigmoid(ga) * ya + _sigmoid(gb) * yb).astype(BF16)
    y = y_ref[...]
    for c0 in range(0, d, MIX_COLS):
        cols = slice(c0, c0 + MIX_COLS)
        merged = _dot(y, wo_ref[:, cols])
        for gi in range(groups):
            rows = slice(gi * ts, (gi + 1) * ts)
            o_ref[gi, :, cols] = x_ref[gi, :, cols] + gt_ref[gi, :, cols] * merged[rows]


def _mix_out(x, oa, ob, gates, gt, wa, wb, wo, tile_rows):
    batch, seq, d = x.shape
    width = oa.shape[1]
    groups, ts, n_tiles, x_map, b_map = _row_tiling(batch, seq, tile_rows)
    tm = groups * ts
    const = lambda i: (0, 0)
    single = pl.Buffered(1)
    est = (2 * 2 * tm * d * 4 + 2 * 2 * tm * width * 2 + 2 * tm * 2 * d * 4 + tm * d * 2
           + (2 * width * d + d * d) * 2 + 6 * tm * MIX_COLS * 4)
    return pl.pallas_call(
        functools.partial(_mix_out_kernel, groups=groups, ts=ts),
        grid=(n_tiles,),
        in_specs=[
            pl.BlockSpec((groups, ts, d), x_map),
            pl.BlockSpec((tm, width), lambda i: (i, 0)),
            pl.BlockSpec((tm, width), lambda i: (i, 0)),
            pl.BlockSpec((tm, 2 * d), lambda i: (i, 0)),
            pl.BlockSpec((groups, 1, d), b_map),
            pl.BlockSpec((width, d), const, pipeline_mode=single),
            pl.BlockSpec((width, d), const, pipeline_mode=single),
            pl.BlockSpec((d, d), const, pipeline_mode=single),
        ],
        out_specs=pl.BlockSpec((groups, ts, d), x_map),
        out_shape=jax.ShapeDtypeStruct((batch, seq, d), F32),
        scratch_shapes=[pltpu.VMEM((tm, d), BF16)],
        compiler_params=pltpu.CompilerParams(
            dimension_semantics=("parallel",), vmem_limit_bytes=_vmem_limit(est)),
        name="mix_out",
    )(x, oa, ob, gates, gt, wa, wb, wo)


FF_TILE = 512


def _ffn_kernel(x_ref, g_ref, sc_ref, sh_ref, gt_ref, gf_ref, wg_ref, wu_ref, wd_ref, o_ref,
                h_ref, acc_ref, *, groups, ts):
    f = pl.program_id(1)

    @pl.when(f == 0)
    def _():
        _norm_mod_rows(x_ref, g_ref, sc_ref, sh_ref, h_ref, groups, ts)

    h = h_ref[...]
    gate = _dot(h, wg_ref[...])
    up = _dot(h, wu_ref[...])
    hidden = (gate * _sigmoid(gate) * up).astype(BF16)
    part = _dot(hidden, wd_ref[...])

    @pl.when(f == 0)
    def _():
        acc_ref[...] = part

    @pl.when(f > 0)
    def _():
        acc_ref[...] += part

    @pl.when(f == pl.num_programs(1) - 1)
    def _():
        gf = gf_ref[...]
        for gi in range(groups):
            gt = gt_ref[gi]

            def body(r, carry, gi=gi, gt=gt):
                r0 = pl.multiple_of(r * NORM_ROWS, NORM_ROWS)
                rows = pl.ds(r0, NORM_ROWS)
                acc_rows = pl.ds(pl.multiple_of(gi * ts + r0, NORM_ROWS), NORM_ROWS)
                x2 = x_ref[gi, rows, :] + gt * acc_ref[acc_rows, :]
                ms = jnp.mean(x2 * x2, axis=-1, keepdims=True)
                o_ref[gi, rows, :] = x2 * lax.rsqrt(ms + EPS) * gf
                return carry

            lax.fori_loop(0, ts // NORM_ROWS, body, 0)


def _ffn(x, g, sc, sh, gt, g_final, w_gate_up, w_down, tile_rows):
    batch, seq, d = x.shape
    d_ff = w_down.shape[0]
    nf = d_ff // FF_TILE
    groups, ts, n_tiles, x_map, b_map = _row_tiling(batch, seq, tile_rows)
    tm = groups * ts
    xm = lambda i, f: x_map(i)
    bm = lambda i, f: b_map(i)
    vec = pl.BlockSpec((1, d), lambda i, f: (0, 0))
    mod = pl.BlockSpec((groups, 1, d), bm)
    est = (2 * 2 * tm * d * 4 + tm * d * (2 + 4) + 2 * 3 * d * FF_TILE * 2 + 8 * tm * FF_TILE * 4
           + tm * d * 4)
    return pl.pallas_call(
        functools.partial(_ffn_kernel, groups=groups, ts=ts),
        grid=(n_tiles, nf),
        in_specs=[
            pl.BlockSpec((groups, ts, d), xm), vec, mod, mod, mod, vec,
            pl.BlockSpec((d, FF_TILE), lambda i, f: (0, f)),
            pl.BlockSpec((d, FF_TILE), lambda i, f: (0, nf + f)),
            pl.BlockSpec((FF_TILE, d), lambda i, f: (f, 0)),
        ],
        out_specs=pl.BlockSpec((groups, ts, d), xm),
        out_shape=jax.ShapeDtypeStruct((batch, seq, d), F32),
        scratch_shapes=[pltpu.VMEM((tm, d), BF16), pltpu.VMEM((tm, d), F32)],
        compiler_params=pltpu.CompilerParams(
            dimension_semantics=("parallel", "arbitrary"), vmem_limit_bytes=_vmem_limit(est)),
        name="ffn",
    )(x, g.reshape(1, d), sc, sh, gt, g_final.reshape(1, d), w_gate_up, w_gate_up, w_down)


def _layer(x, mods, w, attn_a, attn_b, g_final, tile_rows):
    batch, seq, d = x.shape
    sh1, sc1, gt1, sh2, sc2, gt2 = mods
    qa, ka, va, qb, kb, vb, gates = _inproj(x, w["g_mix"], sc1, sh1, w["w_in"], tile_rows)
    oa = attn_a(qa, ka, va)
    ob = attn_b(qb, kb, vb)
    x1 = _mix_out(x, oa, ob, gates, gt1, w["w_a_out"], w["w_b_out"], w["w_o"], min(tile_rows, 256))
    y = _ffn(x1, w["g_ffn"], sc2, sh2, gt2, g_final, w["w_gate_up"], w["w_down"], tile_rows)
    heads_of = lambda t: t.reshape(batch, seq, t.shape[1] // HEAD_DIM, HEAD_DIM)
    return y, heads_of(ka), heads_of(va), heads_of(kb), heads_of(vb)


def kernel(x_prompt, x_sample, c_prompt, c_sample, cache_a_k, cache_a_v, cache_b_k, cache_b_v,
           w_ada, b_ada, g_mix, w_in, rel_bias, w_a_out, w_b_out, w_o, g_ffn, w_gate_up, w_down,
           g_final):
    depth = w_in.shape[0]
    assert depth == 1
    batch, seq, d = x_prompt.shape
    dec_batch, dec_seq, _ = x_sample.shape
    past_len = cache_b_k.shape[2]
    l = 0

    n_c = batch + dec_batch
    c_rows = -(-n_c // 8) * 8
    c_all = jnp.concatenate([c_prompt, c_sample, jnp.zeros((c_rows - n_c, d), F32)], axis=0)
    mod_all = _ada(c_all, w_ada[l], b_ada[l])

    def mods_of(rows):
        return tuple(m[:, None, :] for m in jnp.split(rows, 6, axis=-1))

    mods_p = mods_of(mod_all[:batch])
    mods_s = mods_of(mod_all[batch:n_c])

    w = {
        "g_mix": g_mix[l], "g_ffn": g_ffn[l],
        "w_in": w_in[l].astype(BF16), "w_a_out": w_a_out[l].astype(BF16),
        "w_b_out": w_b_out[l].astype(BF16), "w_o": w_o[l].astype(BF16),
        "w_gate_up": w_gate_up[l].astype(BF16), "w_down": w_down[l].astype(BF16),
    }

    yp, ka, va, kb, vb = _layer(
        x_prompt, mods_p, w,
        functools.partial(_attn_a_prompt, rel_bias=rel_bias[l], batch=batch, seq=seq),
        functools.partial(_attn_b_prompt, batch=batch, seq=seq),
        g_final, 512)
    keep = min(BAND, seq)
    outs_p = (ka[:, -keep:], va[:, -keep:], kb, vb)

    la = cache_a_k.shape[2]
    cak = cache_a_k[l].reshape(dec_batch, la, -1)
    cav = cache_a_v[l].reshape(dec_batch, la, -1)
    cbk = cache_b_k[l].reshape(dec_batch, past_len, -1)
    cbv = cache_b_v[l].reshape(dec_batch, past_len, -1)
    ys, ka_s, va_s, kb_s, vb_s = _layer(
        x_sample, mods_s, w,
        functools.partial(_attn_a_sample, cache_k=cak, cache_v=cav, rel_bias=rel_bias[l],
                          batch=dec_batch, n=dec_seq, past_len=past_len),
        functools.partial(_attn_b_sample, cache_k=cbk, cache_v=cbv, batch=dec_batch, n=dec_seq),
        g_final, dec_batch * dec_seq)
    outs_s = (ka_s, va_s, kb_s, vb_s)

    return (yp, ys) + tuple(t[None] for t in outs_p) + tuple(t[None] for t in outs_s)
```

```python
import functools

import jax
import jax.numpy as jnp
from jax import lax
from jax.experimental import pallas as pl
from jax.experimental.pallas import tpu as pltpu

F32 = jnp.float32
BF16 = jnp.bfloat16

HEAD_DIM = 128
CHUNK = 64
BAND_CHUNKS = 8
BAND = BAND_CHUNKS * CHUNK
REL_CLIP = 128
EPS = 1e-6
ATTN_SCALE = HEAD_DIM ** -0.5
MASK_VALUE = -1e30

V7X_VMEM_BYTES = 64 * 1024 * 1024
V7X_LANES = 128
V7X_BF16_SUBLANES = 16

EXP_ZERO_BELOW = -104.0

NORM_ROWS = 16


def _vmem_limit(estimate_bytes):
    return int(min(V7X_VMEM_BYTES - (4 << 20), max(estimate_bytes * 5 // 4, 16 << 20)))


def _dot(a, b):
    return jnp.dot(a, b, preferred_element_type=F32)


def _dot_nt(a, b):
    return lax.dot_general(a, b, (((1,), (1,)), ((), ())), preferred_element_type=F32)


def _sigmoid(x):
    return 1.0 / (1.0 + jnp.exp(-x))


def _ada_kernel(c_ref, w_ref, b_ref, o_ref):
    c = c_ref[...]
    a = (c * _sigmoid(c)).astype(BF16)
    o_ref[...] = _dot(a, w_ref[...].astype(BF16)) + b_ref[...]


def _ada(c_all, w_ada, b_ada):
    rows, d = c_all.shape
    n = w_ada.shape[1]
    tn = 1024
    est = 2 * d * tn * 4 + d * tn * 2 + 4 * rows * (d + tn) * 4
    return pl.pallas_call(
        _ada_kernel,
        grid=(n // tn,),
        in_specs=[
            pl.BlockSpec((rows, d), lambda j: (0, 0)),
            pl.BlockSpec((d, tn), lambda j: (0, j)),
            pl.BlockSpec((1, tn), lambda j: (0, j)),
        ],
        out_specs=pl.BlockSpec((rows, tn), lambda j: (0, j)),
        out_shape=jax.ShapeDtypeStruct((rows, n), F32),
        compiler_params=pltpu.CompilerParams(
            dimension_semantics=("parallel",), vmem_limit_bytes=_vmem_limit(est)),
        name="ada",
    )(c_all, w_ada, b_ada.reshape(1, n))


def _norm_mod_rows(x_ref, g_ref, sc_ref, sh_ref, h_ref, groups, ts):
    gvec = g_ref[...]
    for gi in range(groups):
        scale = gvec * (1.0 + sc_ref[gi])
        shift = sh_ref[gi]

        def body(r, carry, gi=gi, scale=scale, shift=shift):
            r0 = pl.multiple_of(r * NORM_ROWS, NORM_ROWS)
            x = x_ref[gi, pl.ds(r0, NORM_ROWS), :]
            ms = jnp.mean(x * x, axis=-1, keepdims=True)
            h = x * lax.rsqrt(ms + EPS) * scale + shift
            h_ref[pl.ds(pl.multiple_of(gi * ts + r0, NORM_ROWS), NORM_ROWS), :] = h.astype(BF16)
            return carry

        lax.fori_loop(0, ts // NORM_ROWS, body, 0)


def _row_tiling(batch, seq, tile_rows):
    if seq >= tile_rows:
        assert seq % tile_rows == 0
        groups, ts, ns = 1, tile_rows, seq // tile_rows
    else:
        assert tile_rows % seq == 0 and batch % (tile_rows // seq) == 0
        groups, ts, ns = tile_rows // seq, seq, 1
    n_tiles = batch * seq // (groups * ts)
    if groups == 1:
        x_map = lambda i: (i // ns, i % ns, 0)
        b_map = lambda i: (i // ns, 0, 0)
    else:
        x_map = lambda i: (i, 0, 0)
        b_map = lambda i: (i, 0, 0)
    return groups, ts, n_tiles, x_map, b_map


N_QKV = 6
IN_TILE = 512


def _inproj_kernel(x_ref, g_ref, sc_ref, sh_ref, w_ref,
                   qa_ref, ka_ref, va_ref, qb_ref, kb_ref, vb_ref, gates_ref, h_ref,
                   *, groups, ts, tiles_per_group):
    j = pl.program_id(1)

    @pl.when(j == 0)
    def _():
        _norm_mod_rows(x_ref, g_ref, sc_ref, sh_ref, h_ref, groups, ts)

    for idx, o_ref in enumerate((qa_ref, ka_ref, va_ref, qb_ref, kb_ref, vb_ref)):
        @pl.when(j // tiles_per_group == idx)
        def _(o_ref=o_ref):
            o_ref[...] = _dot(h_ref[...], w_ref[...]).astype(o_ref.dtype)

    @pl.when(j >= N_QKV * tiles_per_group)
    def _():
        gates_ref[...] = _dot(h_ref[...], w_ref[...]).astype(gates_ref.dtype)


def _inproj(x, g, sc, sh, w_in, tile_rows, width):
    batch, seq, d = x.shape
    rows = batch * seq
    n_cols = w_in.shape[1]
    tpg = width // IN_TILE
    nj = n_cols // IN_TILE
    n_gate_tiles = nj - N_QKV * tpg
    groups, ts, n_tiles, x_map, b_map = _row_tiling(batch, seq, tile_rows)
    tm = groups * ts
    xm = lambda i, j: x_map(i)
    bm = lambda i, j: b_map(i)

    def group_spec(idx):
        return pl.BlockSpec((tm, IN_TILE), lambda i, j: (i, jnp.clip(j - idx * tpg, 0, tpg - 1)))

    est = (2 * tm * d * 4 + 2 * d * IN_TILE * 2 + tm * d * 2
           + 2 * tm * IN_TILE * (2 + 4 + 4 + 2 + 4 + 4 + 2) + 2 * tm * IN_TILE * 4)
    out_dtypes = (BF16, F32, F32, BF16, F32, F32)
    return pl.pallas_call(
        functools.partial(_inproj_kernel, groups=groups, ts=ts, tiles_per_group=tpg),
        grid=(n_tiles, nj),
        in_specs=[
            pl.BlockSpec((groups, ts, d), xm),
            pl.BlockSpec((1, d), lambda i, j: (0, 0)),
            pl.BlockSpec((groups, 1, d), bm),
            pl.BlockSpec((groups, 1, d), bm),
            pl.BlockSpec((d, IN_TILE), lambda i, j: (0, j)),
        ],
        out_specs=[group_spec(idx) for idx in range(N_QKV)] + [
            pl.BlockSpec((tm, IN_TILE), lambda i, j: (i, jnp.maximum(j - N_QKV * tpg, 0)))],
        out_shape=[jax.ShapeDtypeStruct((rows, width), dt) for dt in out_dtypes]
        + [jax.ShapeDtypeStruct((rows, n_gate_tiles * IN_TILE), BF16)],
        scratch_shapes=[pltpu.VMEM((tm, d), BF16)],
        compiler_params=pltpu.CompilerParams(
            dimension_semantics=("parallel", "arbitrary"), vmem_limit_bytes=_vmem_limit(est)),
        name="inproj",
    )(x, g.reshape(1, d), sc, sh, w_in)


A_BLOCK = 256


A_HEADS = 2
TOEPLITZ_ROW = 1024


def _toeplitz_row(rel_bias, offset, n_cols):
    m = jnp.arange(TOEPLITZ_ROW)
    m = jnp.where(m < n_cols, m, m - TOEPLITZ_ROW)
    idx = jnp.clip(offset - m, -REL_CLIP, REL_CLIP) + REL_CLIP
    return rel_bias.astype(F32)[:, None, idx]


def _toeplitz_bias(row, n_rows):
    return pltpu.roll(jnp.broadcast_to(row, (n_rows, TOEPLITZ_ROW)), 0, 1, stride=1, stride_axis=0)


def _attn_a_prompt_kernel(g_ref, q_ref, k0_ref, k1_ref, k2_ref, v0_ref, v1_ref, v2_ref, o_ref, bias_ref):
    qi = pl.program_id(2)
    n_keys = 3 * A_BLOCK

    @pl.when(qi == 0)
    def _():
        qc = lax.broadcasted_iota(jnp.int32, (A_BLOCK, n_keys), 0) // CHUNK
        kc = lax.broadcasted_iota(jnp.int32, (A_BLOCK, n_keys), 1) // CHUNK - BAND // CHUNK
        in_band = jnp.logical_and(kc <= qc, kc >= qc - BAND_CHUNKS)
        for hh in range(A_HEADS):
            rel = _toeplitz_bias(g_ref[hh], A_BLOCK)[:, :n_keys]
            bias_ref[hh] = jnp.where(in_band, rel, MASK_VALUE)

    for hh in range(A_HEADS):
        cols = slice(hh * HEAD_DIM, (hh + 1) * HEAD_DIM)
        q = q_ref[:, cols]
        scores = []
        for jb, k_ref in enumerate((k0_ref, k1_ref, k2_ref)):
            s = _dot_nt(q, k_ref[:, cols].astype(BF16)) * ATTN_SCALE
            s = s + bias_ref[hh, :, jb * A_BLOCK:(jb + 1) * A_BLOCK]
            scores.append(jnp.where(qi + (jb - 2) >= 0, s, MASK_VALUE))
        m = functools.reduce(jnp.maximum, [jnp.max(s, axis=-1, keepdims=True) for s in scores])
        es = [jnp.exp(s - m) for s in scores]
        l = functools.reduce(jnp.add, [jnp.sum(e, axis=-1, keepdims=True) for e in es])
        acc = functools.reduce(jnp.add, [
            _dot(e.astype(BF16), v_ref[:, cols].astype(BF16))
            for e, v_ref in zip(es, (v0_ref, v1_ref, v2_ref))])
        o_ref[:, cols] = (acc / l).astype(o_ref.dtype)


def _attn_a_prompt(q, k, v, rel_bias, batch, seq):
    rows, width = q.shape
    heads = width // HEAD_DIM
    nq = seq // A_BLOCK
    assert BAND == 2 * A_BLOCK and seq % A_BLOCK == 0 and heads % A_HEADS == 0
    assert 4 * A_BLOCK - 1 <= TOEPLITZ_ROW
    g = _toeplitz_row(rel_bias, BAND, 3 * A_BLOCK)
    blk = (A_BLOCK, A_HEADS * HEAD_DIM)

    def kv_spec(jb):
        return pl.BlockSpec(blk, lambda b, h, i: (b * nq + jnp.maximum(i + (jb - 2), 0), h))

    est = (2 * A_BLOCK * A_HEADS * HEAD_DIM * (2 + 6 * 4 + 2)
           + A_HEADS * (A_BLOCK * 3 * A_BLOCK * 4 + 8 * A_BLOCK * 3 * A_BLOCK * 4))
    return pl.pallas_call(
        _attn_a_prompt_kernel,
        grid=(batch, heads // A_HEADS, nq),
        in_specs=[pl.BlockSpec((A_HEADS, 1, TOEPLITZ_ROW), lambda b, h, i: (h, 0, 0)),
                  pl.BlockSpec(blk, lambda b, h, i: (b * nq + i, h)),
                  kv_spec(0), kv_spec(1), kv_spec(2), kv_spec(0), kv_spec(1), kv_spec(2)],
        out_specs=pl.BlockSpec(blk, lambda b, h, i: (b * nq + i, h)),
        out_shape=jax.ShapeDtypeStruct((rows, width), BF16),
        scratch_shapes=[pltpu.VMEM((A_HEADS, A_BLOCK, 3 * A_BLOCK), F32)],
        compiler_params=pltpu.CompilerParams(
            dimension_semantics=("parallel", "parallel", "arbitrary"),
            vmem_limit_bytes=_vmem_limit(est)),
        name="attn_a_prompt",
    )(g, q, k, k, k, v, v, v)


def _attn_a_sample_kernel(g_ref, q_ref, kc_ref, vc_ref, kn_ref, vn_ref, o_ref):
    q = q_ref[...]
    n = q.shape[0]
    lc = kc_ref.shape[1]
    bias = _toeplitz_bias(g_ref[0], n)
    sc = _dot_nt(q, kc_ref[0].astype(BF16)) * ATTN_SCALE + bias[:, :lc]
    sn = _dot_nt(q, kn_ref[...].astype(BF16)) * ATTN_SCALE + bias[:, lc:lc + n]
    m = jnp.maximum(jnp.max(sc, axis=-1, keepdims=True), jnp.max(sn, axis=-1, keepdims=True))
    ec = jnp.exp(sc - m)
    en = jnp.exp(sn - m)
    l = jnp.sum(ec, axis=-1, keepdims=True) + jnp.sum(en, axis=-1, keepdims=True)
    acc = _dot(ec.astype(BF16), vc_ref[0].astype(BF16)) + _dot(en.astype(BF16), vn_ref[...].astype(BF16))
    o_ref[...] = (acc / l).astype(o_ref.dtype)


def _attn_a_sample(q, k, v, cache_k, cache_v, rel_bias, batch, n, past_len):
    rows, width = q.shape
    heads = width // HEAD_DIM
    lc = cache_k.shape[1]
    assert past_len >= lc and lc + 2 * n - 1 <= TOEPLITZ_ROW and lc % V7X_LANES == 0
    g = _toeplitz_row(rel_bias, lc, lc + n)
    blk = (n, HEAD_DIM)
    cblk = (1, lc, HEAD_DIM)
    est = 2 * (2 * lc * HEAD_DIM * 4 + n * lc * 4) + 8 * n * lc * 4
    return pl.pallas_call(
        _attn_a_sample_kernel,
        grid=(batch, heads),
        in_specs=[pl.BlockSpec((1, 1, TOEPLITZ_ROW), lambda b, h: (h, 0, 0)),
                  pl.BlockSpec(blk, lambda b, h: (b, h)),
                  pl.BlockSpec(cblk, lambda b, h: (b, 0, h)),
                  pl.BlockSpec(cblk, lambda b, h: (b, 0, h)),
                  pl.BlockSpec(blk, lambda b, h: (b, h)),
                  pl.BlockSpec(blk, lambda b, h: (b, h))],
        out_specs=pl.BlockSpec(blk, lambda b, h: (b, h)),
        out_shape=jax.ShapeDtypeStruct((rows, width), BF16),
        compiler_params=pltpu.CompilerParams(
            dimension_semantics=("parallel", "parallel"), vmem_limit_bytes=_vmem_limit(est)),
        name="attn_a_sample",
    )(g, q, cache_k, cache_v, k, v)


B_BLOCK = 256


def _later_matrix(n):
    j = jnp.arange(n)[:, None]
    s = jnp.arange(n)[None, :]
    return (j > s).astype(BF16)


def _strict_causal_mask(n):
    t = jnp.arange(n)[:, None]
    s = jnp.arange(n)[None, :]
    return jnp.where(s < t, 0.0, MASK_VALUE).astype(F32)


def _stick_terms(q, k, later_mat, z_mask=None, valid=None):
    z = _dot_nt(q, k) * ATTN_SCALE
    if z_mask is not None:
        z = z + z_mask
    if valid is not None:
        z = jnp.where(valid, z, MASK_VALUE)
    log_1m = -(jnp.maximum(z, 0.0) + jnp.log(1.0 + jnp.exp(-jnp.abs(z))))
    hi = log_1m.astype(BF16)
    lo = (log_1m - hi.astype(F32)).astype(BF16)
    later = _dot(hi, later_mat) + _dot(lo, later_mat)
    return z + log_1m + later, jnp.sum(log_1m, axis=-1, keepdims=True)


def _stick_sweep_past(q, k_ref, v_ref, n_blocks, tk, c, acc, later_mat):
    def cond(carry):
        j, c, _ = carry
        return jnp.logical_and(j >= 0, jnp.max(c) > EXP_ZERO_BELOW)

    def body(carry):
        j, c, acc = carry
        rows = pl.ds(pl.multiple_of(j * tk, tk), tk)
        logit, row_sum = _stick_terms(q, k_ref[rows, :].astype(BF16), later_mat)
        acc = acc + _dot(jnp.exp(logit + c).astype(BF16), v_ref[rows, :].astype(BF16))
        return j - 1, c + row_sum, acc

    _, _, acc = lax.while_loop(cond, body, (n_blocks - 1, c, acc))
    return acc


def _attn_b_prompt_kernel(q_ref, k_ref, v_ref, u_ref, mask_ref, o_ref):
    qi = pl.program_id(2)
    q = q_ref[...]
    k_ref = k_ref.at[0]
    v_ref = v_ref.at[0]
    later_mat = u_ref[...]
    rows0 = pl.ds(pl.multiple_of(qi * B_BLOCK, B_BLOCK), B_BLOCK)
    rows1 = pl.ds(pl.multiple_of(jnp.maximum(qi - 1, 0) * B_BLOCK, B_BLOCK), B_BLOCK)
    logit0, sum0 = _stick_terms(q, k_ref[rows0, :].astype(BF16), later_mat, z_mask=mask_ref[...])
    logit1, sum1 = _stick_terms(q, k_ref[rows1, :].astype(BF16), later_mat, valid=qi > 0)
    acc = (_dot(jnp.exp(logit0).astype(BF16), v_ref[rows0, :].astype(BF16))
           + _dot(jnp.exp(logit1 + sum0).astype(BF16), v_ref[rows1, :].astype(BF16)))
    acc = _stick_sweep_past(q, k_ref, v_ref, qi - 1, B_BLOCK, sum0 + sum1, acc, later_mat)
    o_ref[...] = acc.astype(o_ref.dtype)


def _attn_b_prompt(q, k, v, batch, seq):
    rows, width = q.shape
    heads = width // HEAD_DIM
    nq = seq // B_BLOCK
    k3 = k.reshape(batch, seq, width)
    v3 = v.reshape(batch, seq, width)
    blk = (B_BLOCK, HEAD_DIM)
    kv_spec = pl.BlockSpec((1, seq, HEAD_DIM), lambda b, h, i: (b, 0, h))
    const = pl.BlockSpec((B_BLOCK, B_BLOCK), lambda b, h, i: (0, 0))
    est = 2 * 2 * seq * HEAD_DIM * 4 + 40 * B_BLOCK * B_BLOCK * 4
    return pl.pallas_call(
        _attn_b_prompt_kernel,
        grid=(batch, heads, nq),
        in_specs=[pl.BlockSpec(blk, lambda b, h, i: (b * nq + i, h)), kv_spec, kv_spec, const, const],
        out_specs=pl.BlockSpec(blk, lambda b, h, i: (b * nq + i, h)),
        out_shape=jax.ShapeDtypeStruct((rows, width), BF16),
        compiler_params=pltpu.CompilerParams(
            dimension_semantics=("parallel", "parallel", "parallel"),
            vmem_limit_bytes=_vmem_limit(est)),
        name="attn_b_prompt",
    )(q, k3, v3, _later_matrix(B_BLOCK), _strict_causal_mask(B_BLOCK))


def _attn_b_sample_kernel(q_ref, kc_ref, vc_ref, kn_ref, vn_ref, un_ref, mask_ref, u_ref, o_ref,
                          *, n_blocks):
    q = q_ref[...]
    kc_ref = kc_ref.at[0]
    vc_ref = vc_ref.at[0]
    later_mat = u_ref[...]
    rows1 = pl.ds((n_blocks - 1) * B_BLOCK, B_BLOCK)
    logit0, sum0 = _stick_terms(q, kn_ref[...].astype(BF16), un_ref[...], z_mask=mask_ref[...])
    logit1, sum1 = _stick_terms(q, kc_ref[rows1, :].astype(BF16), later_mat)
    acc = (_dot(jnp.exp(logit0).astype(BF16), vn_ref[...].astype(BF16))
           + _dot(jnp.exp(logit1 + sum0).astype(BF16), vc_ref[rows1, :].astype(BF16)))
    acc = _stick_sweep_past(q, kc_ref, vc_ref, n_blocks - 1, B_BLOCK, sum0 + sum1, acc, later_mat)
    o_ref[...] = acc.astype(o_ref.dtype)


def _attn_b_sample(q, k, v, cache_k, cache_v, batch, n):
    rows, width = q.shape
    heads = width // HEAD_DIM
    lc = cache_k.shape[1]
    assert lc % B_BLOCK == 0 and lc >= B_BLOCK
    blk = (n, HEAD_DIM)
    cblk = (1, lc, HEAD_DIM)
    const = lambda shape: pl.BlockSpec(shape, lambda b, h: (0, 0))
    est = 2 * 2 * lc * HEAD_DIM * 4 + 40 * n * B_BLOCK * 4 + 8 * B_BLOCK * B_BLOCK
    return pl.pallas_call(
        functools.partial(_attn_b_sample_kernel, n_blocks=lc // B_BLOCK),
        grid=(batch, heads),
        in_specs=[pl.BlockSpec(blk, lambda b, h: (b, h)),
                  pl.BlockSpec(cblk, lambda b, h: (b, 0, h)),
                  pl.BlockSpec(cblk, lambda b, h: (b, 0, h)),
                  pl.BlockSpec(blk, lambda b, h: (b, h)),
                  pl.BlockSpec(blk, lambda b, h: (b, h)),
                  const((n, n)), const((n, n)), const((B_BLOCK, B_BLOCK))],
        out_specs=pl.BlockSpec(blk, lambda b, h: (b, h)),
        out_shape=jax.ShapeDtypeStruct((rows, width), BF16),
        compiler_params=pltpu.CompilerParams(
            dimension_semantics=("parallel", "parallel"), vmem_limit_bytes=_vmem_limit(est)),
        name="attn_b_sample",
    )(q, cache_k, cache_v, k, v, _later_matrix(n), _strict_causal_mask(n), _later_matrix(B_BLOCK))


MIX_COLS = 512


def _mix_out_kernel(x_ref, oa_ref, ob_ref, gates_ref, gt_ref, wa_ref, wb_ref, wo_ref, o_ref, y_ref,
                    *, groups, ts):
    d = o_ref.shape[-1]
    oa = oa_ref[...]
    ob = ob_ref[...]
    for c0 in range(0, d, MIX_COLS):
        cols = slice(c0, c0 + MIX_COLS)
        ya = _dot(oa, wa_ref[:, cols])
        yb = _dot(ob, wb_ref[:, cols])
        ga = gates_ref[:, cols].astype(F32)
        gb = gates_ref[:, d + c0:d + c0 + MIX_COLS].astype(F32)
        y_ref[:, cols] = (_sigmoid(ga) * ya + _sigmoid(gb) * yb).astype(BF16)
    y = y_ref[...]
    for c0 in range(0, d, MIX_COLS):
        cols = slice(c0, c0 + MIX_COLS)
        merged = _dot(y, wo_ref[:, cols])
        for gi in range(groups):
            rows = slice(gi * ts, (gi + 1) * ts)
            o_ref[gi, :, cols] = x_ref[gi, :, cols] + gt_ref[gi, :, cols] * merged[rows]


def _mix_out(x, oa, ob, gates, gt, wa, wb, wo, tile_rows):
    batch, seq, d = x.shape
    width = oa.shape[1]
    groups, ts, n_tiles, x_map, b_map = _row_tiling(batch, seq, tile_rows)
    tm = groups * ts
    const = lambda i: (0, 0)
    single = pl.Buffered(1)
    est = (2 * 2 * tm * d * 4 + 2 * 2 * tm * width * 2 + 2 * tm * 2 * d * 2 + tm * d * 2
           + (2 * width * d + d * d) * 2 + 6 * tm * MIX_COLS * 4)
    return pl.pallas_call(
        functools.partial(_mix_out_kernel, groups=groups, ts=ts),
        grid=(n_tiles,),
        in_specs=[
            pl.BlockSpec((groups, ts, d), x_map),
            pl.BlockSpec((tm, width), lambda i: (i, 0)),
            pl.BlockSpec((tm, width), lambda i: (i, 0)),
            pl.BlockSpec((tm, 2 * d), lambda i: (i, 0)),
            pl.BlockSpec((groups, 1, d), b_map),
            pl.BlockSpec((width, d), const, pipeline_mode=single),
            pl.BlockSpec((width, d), const, pipeline_mode=single),
            pl.BlockSpec((d, d), const, pipeline_mode=single),
        ],
        out_specs=pl.BlockSpec((groups, ts, d), x_map),
        out_shape=jax.ShapeDtypeStruct((batch, seq, d), F32),
        scratch_shapes=[pltpu.VMEM((tm, d), BF16)],
        compiler_params=pltpu.CompilerParams(
            dimension_semantics=("parallel",), vmem_limit_bytes=_vmem_limit(est)),
        name="mix_out",
    )(x, oa, ob, gates, gt, wa, wb, wo)


FF_TILE = 512


def _ffn_kernel(x_ref, g_ref, sc_ref, sh_ref, gt_ref, gf_ref, wg_ref, wu_ref, wd_ref, o_ref,
                h_ref, acc_ref, *, groups, ts):
    f = pl.program_id(1)

    @pl.when(f == 0)
    def _():
        _norm_mod_rows(x_ref, g_ref, sc_ref, sh_ref, h_ref, groups, ts)
        acc_ref[...] = jnp.zeros_like(acc_ref)

    h = h_ref[...]
    gate = _dot(h, wg_ref[...])
    up = _dot(h, wu_ref[...])
    hidden = (gate * _sigmoid(gate) * up).astype(BF16)
    acc_ref[...] += _dot(hidden, wd_ref[...])

    @pl.when(f == pl.num_programs(1) - 1)
    def _():
        gf = gf_ref[...]
        for gi in range(groups):
            gt = gt_ref[gi]

            def body(r, carry, gi=gi, gt=gt):
                r0 = pl.multiple_of(r * NORM_ROWS, NORM_ROWS)
                rows = pl.ds(r0, NORM_ROWS)
                acc_rows = pl.ds(pl.multiple_of(gi * ts + r0, NORM_ROWS), NORM_ROWS)
                x2 = x_ref[gi, rows, :] + gt * acc_ref[acc_rows, :]
                ms = jnp.mean(x2 * x2, axis=-1, keepdims=True)
                o_ref[gi, rows, :] = x2 * lax.rsqrt(ms + EPS) * gf
                return carry

            lax.fori_loop(0, ts // NORM_ROWS, body, 0)


def _ffn(x, g, sc, sh, gt, g_final, w_gate_up, w_down, tile_rows):
    batch, seq, d = x.shape
    d_ff = w_down.shape[0]
    nf = d_ff // FF_TILE
    groups, ts, n_tiles, x_map, b_map = _row_tiling(batch, seq, tile_rows)
    tm = groups * ts
    xm = lambda i, f: x_map(i)
    bm = lambda i, f: b_map(i)
    vec = pl.BlockSpec((1, d), lambda i, f: (0, 0))
    mod = pl.BlockSpec((groups, 1, d), bm)
    est = (2 * 2 * tm * d * 4 + tm * d * (2 + 4) + 2 * 3 * d * FF_TILE * 2 + 8 * tm * FF_TILE * 4
           + tm * d * 4)
    return pl.pallas_call(
        functools.partial(_ffn_kernel, groups=groups, ts=ts),
        grid=(n_tiles, nf),
        in_specs=[
            pl.BlockSpec((groups, ts, d), xm), vec, mod, mod, mod, vec,
            pl.BlockSpec((d, FF_TILE), lambda i, f: (0, f)),
            pl.BlockSpec((d, FF_TILE), lambda i, f: (0, nf + f)),
            pl.BlockSpec((FF_TILE, d), lambda i, f: (f, 0)),
        ],
        out_specs=pl.BlockSpec((groups, ts, d), xm),
        out_shape=jax.ShapeDtypeStruct((batch, seq, d), F32),
        scratch_shapes=[pltpu.VMEM((tm, d), BF16), pltpu.VMEM((tm, d), F32)],
        compiler_params=pltpu.CompilerParams(
            dimension_semantics=("parallel", "arbitrary"), vmem_limit_bytes=_vmem_limit(est)),
        name="ffn",
    )(x, g.reshape(1, d), sc, sh, gt, g_final.reshape(1, d), w_gate_up, w_gate_up, w_down)


def _layer(x, mods, w, attn_a, attn_b, g_final, in_rows, mix_rows, ffn_rows):
    batch, seq, d = x.shape
    sh1, sc1, gt1, sh2, sc2, gt2 = mods
    width = w["w_a_out"].shape[0]
    qa, ka, va, qb, kb, vb, gates = _inproj(x, w["g_mix"], sc1, sh1, w["w_in"], in_rows, width)
    oa = attn_a(qa, ka, va)
    ob = attn_b(qb, kb, vb)
    x1 = _mix_out(x, oa, ob, gates, gt1, w["w_a_out"], w["w_b_out"], w["w_o"], mix_rows)
    y = _ffn(x1, w["g_ffn"], sc2, sh2, gt2, g_final, w["w_gate_up"], w["w_down"], ffn_rows)
    heads_of = lambda t: t.reshape(batch, seq, t.shape[1] // HEAD_DIM, HEAD_DIM)
    return y, heads_of(ka), heads_of(va), heads_of(kb), heads_of(vb)


def kernel(x_prompt, x_sample, c_prompt, c_sample, cache_a_k, cache_a_v, cache_b_k, cache_b_v,
           w_ada, b_ada, g_mix, w_in, rel_bias, w_a_out, w_b_out, w_o, g_ffn, w_gate_up, w_down,
           g_final):
    depth = w_in.shape[0]
    assert depth == 1
    batch, seq, d = x_prompt.shape
    dec_batch, dec_seq, _ = x_sample.shape
    past_len = cache_b_k.shape[2]
    l = 0

    n_c = batch + dec_batch
    c_rows = -(-n_c // 8) * 8
    c_all = jnp.concatenate([c_prompt, c_sample, jnp.zeros((c_rows - n_c, d), F32)], axis=0)
    mod_all = _ada(c_all, w_ada[l], b_ada[l])

    def mods_of(rows):
        return tuple(m[:, None, :] for m in jnp.split(rows, 6, axis=-1))

    mods_p = mods_of(mod_all[:batch])
    mods_s = mods_of(mod_all[batch:n_c])

    w = {
        "g_mix": g_mix[l], "g_ffn": g_ffn[l],
        "w_in": w_in[l].astype(BF16), "w_a_out": w_a_out[l].astype(BF16),
        "w_b_out": w_b_out[l].astype(BF16), "w_o": w_o[l].astype(BF16),
        "w_gate_up": w_gate_up[l].astype(BF16), "w_down": w_down[l].astype(BF16),
    }

    yp, ka, va, kb, vb = _layer(
        x_prompt, mods_p, w,
        functools.partial(_attn_a_prompt, rel_bias=rel_bias[l], batch=batch, seq=seq),
        functools.partial(_attn_b_prompt, batch=batch, seq=seq),
        g_final, in_rows=1024, mix_rows=256, ffn_rows=512)
    keep = min(BAND, seq)
    outs_p = (ka[:, -keep:], va[:, -keep:], kb, vb)

    la = cache_a_k.shape[2]
    cak = cache_a_k[l].reshape(dec_batch, la, -1)
    cav = cache_a_v[l].reshape(dec_batch, la, -1)
    cbk = cache_b_k[l].reshape(dec_batch, past_len, -1)
    cbv = cache_b_v[l].reshape(dec_batch, past_len, -1)
    ys, ka_s, va_s, kb_s, vb_s = _layer(
        x_sample, mods_s, w,
        functools.partial(_attn_a_sample, cache_k=cak, cache_v=cav, rel_bias=rel_bias[l],
                          batch=dec_batch, n=dec_seq, past_len=past_len),
        functools.partial(_attn_b_sample, cache_k=cbk, cache_v=cbv, batch=dec_batch, n=dec_seq),
        g_final, in_rows=dec_batch * dec_seq, mix_rows=dec_batch * dec_seq,
        ffn_rows=dec_batch * dec_seq)
    outs_s = (ka_s, va_s, kb_s, vb_s)

    return (yp, ys) + tuple(t[None] for t in outs_p) + tuple(t[None] for t in outs_s)
```

```python
import functools

import jax
import jax.numpy as jnp
from jax import lax
from jax.experimental import pallas as pl
from jax.experimental.pallas import tpu as pltpu

F32 = jnp.float32
BF16 = jnp.bfloat16

HEAD_DIM = 128
CHUNK = 64
BAND_CHUNKS = 8
BAND = BAND_CHUNKS * CHUNK
REL_CLIP = 128
EPS = 1e-6
ATTN_SCALE = HEAD_DIM ** -0.5
LOG2_E = 1.4426950408889634
MASK_VALUE = -1e30

V7X_VMEM_BYTES = 64 * 1024 * 1024
V7X_LANES = 128
V7X_BF16_SUBLANES = 16

EXP_ZERO_BELOW = -104.0

NORM_ROWS = 16
NORM_UNROLL = 8


def _norm_unroll(n_rows):
    steps = n_rows // NORM_ROWS
    return NORM_UNROLL if steps % NORM_UNROLL == 0 else 1


def _vmem_limit(estimate_bytes):
    return int(min(V7X_VMEM_BYTES - (4 << 20), max(estimate_bytes * 5 // 4, 16 << 20)))


def _dot(a, b):
    return jnp.dot(a, b, preferred_element_type=F32)


def _dot_nt(a, b):
    return lax.dot_general(a, b, (((1,), (1,)), ((), ())), preferred_element_type=F32)


def _sigmoid(x):
    return 1.0 / (1.0 + jnp.exp(-x))


def _ada_kernel(c_ref, w_ref, b_ref, o_ref):
    c = c_ref[...]
    a = (c * _sigmoid(c)).astype(BF16)
    o_ref[...] = _dot(a, w_ref[...].astype(BF16)) + b_ref[...]


def _ada(c_all, w_ada, b_ada):
    rows, d = c_all.shape
    n = w_ada.shape[1]
    tn = 1024
    est = 2 * d * tn * 4 + d * tn * 2 + 4 * rows * (d + tn) * 4
    return pl.pallas_call(
        _ada_kernel,
        grid=(n // tn,),
        in_specs=[
            pl.BlockSpec((rows, d), lambda j: (0, 0)),
            pl.BlockSpec((d, tn), lambda j: (0, j)),
            pl.BlockSpec((1, tn), lambda j: (0, j)),
        ],
        out_specs=pl.BlockSpec((rows, tn), lambda j: (0, j)),
        out_shape=jax.ShapeDtypeStruct((rows, n), F32),
        compiler_params=pltpu.CompilerParams(
            dimension_semantics=("parallel",), vmem_limit_bytes=_vmem_limit(est)),
        name="ada",
    )(c_all, w_ada, b_ada.reshape(1, n))


def _norm_mod_rows(x_ref, g_ref, sc_ref, sh_ref, h_ref, groups, ts):
    gvec = g_ref[...]
    for gi in range(groups):
        scale = gvec * (1.0 + sc_ref[gi])
        shift = sh_ref[gi]

        def body(r, carry, gi=gi, scale=scale, shift=shift):
            r0 = pl.multiple_of(r * NORM_ROWS, NORM_ROWS)
            x = x_ref[gi, pl.ds(r0, NORM_ROWS), :]
            ms = jnp.mean(x * x, axis=-1, keepdims=True)
            h = x * lax.rsqrt(ms + EPS) * scale + shift
            h_ref[pl.ds(pl.multiple_of(gi * ts + r0, NORM_ROWS), NORM_ROWS), :] = h.astype(BF16)
            return carry

        lax.fori_loop(0, ts // NORM_ROWS, body, 0, unroll=_norm_unroll(ts))


def _row_tiling(batch, seq, tile_rows):
    if seq >= tile_rows:
        assert seq % tile_rows == 0
        groups, ts, ns = 1, tile_rows, seq // tile_rows
    else:
        assert tile_rows % seq == 0 and batch % (tile_rows // seq) == 0
        groups, ts, ns = tile_rows // seq, seq, 1
    n_tiles = batch * seq // (groups * ts)
    if groups == 1:
        x_map = lambda i: (i // ns, i % ns, 0)
        b_map = lambda i: (i // ns, 0, 0)
    else:
        x_map = lambda i: (i, 0, 0)
        b_map = lambda i: (i, 0, 0)
    return groups, ts, n_tiles, x_map, b_map


N_QKV = 6
IN_TILE = 512


def _inproj_kernel(x_ref, g_ref, sc_ref, sh_ref, w_ref,
                   qa_ref, ka_ref, va_ref, qb_ref, kb_ref, vb_ref, gates_ref, h_ref,
                   *, groups, ts, tiles_per_group):
    j = pl.program_id(1)

    @pl.when(j == 0)
    def _():
        _norm_mod_rows(x_ref, g_ref, sc_ref, sh_ref, h_ref, groups, ts)

    scales = (ATTN_SCALE * LOG2_E, None, None, ATTN_SCALE, None, None)
    for idx, o_ref in enumerate((qa_ref, ka_ref, va_ref, qb_ref, kb_ref, vb_ref)):
        @pl.when(j // tiles_per_group == idx)
        def _(o_ref=o_ref, scale=scales[idx]):
            p = _dot(h_ref[...], w_ref[...])
            if scale is not None:
                p = p * scale
            o_ref[...] = p.astype(o_ref.dtype)

    @pl.when(j >= N_QKV * tiles_per_group)
    def _():
        gates_ref[...] = _dot(h_ref[...], w_ref[...]).astype(gates_ref.dtype)


def _inproj(x, g, sc, sh, w_in, tile_rows, width):
    batch, seq, d = x.shape
    rows = batch * seq
    n_cols = w_in.shape[1]
    tpg = width // IN_TILE
    nj = n_cols // IN_TILE
    n_gate_tiles = nj - N_QKV * tpg
    groups, ts, n_tiles, x_map, b_map = _row_tiling(batch, seq, tile_rows)
    tm = groups * ts
    xm = lambda i, j: x_map(i)
    bm = lambda i, j: b_map(i)

    def group_spec(idx):
        return pl.BlockSpec((tm, IN_TILE), lambda i, j: (i, jnp.clip(j - idx * tpg, 0, tpg - 1)))

    est = (2 * tm * d * 4 + 2 * d * IN_TILE * 2 + tm * d * 2
           + 2 * tm * IN_TILE * (2 + 4 + 4 + 2 + 4 + 4 + 2) + 2 * tm * IN_TILE * 4)
    out_dtypes = (BF16, F32, F32, BF16, F32, F32)
    return pl.pallas_call(
        functools.partial(_inproj_kernel, groups=groups, ts=ts, tiles_per_group=tpg),
        grid=(n_tiles, nj),
        in_specs=[
            pl.BlockSpec((groups, ts, d), xm),
            pl.BlockSpec((1, d), lambda i, j: (0, 0)),
            pl.BlockSpec((groups, 1, d), bm),
            pl.BlockSpec((groups, 1, d), bm),
            pl.BlockSpec((d, IN_TILE), lambda i, j: (0, j)),
        ],
        out_specs=[group_spec(idx) for idx in range(N_QKV)] + [
            pl.BlockSpec((tm, IN_TILE), lambda i, j: (i, jnp.maximum(j - N_QKV * tpg, 0)))],
        out_shape=[jax.ShapeDtypeStruct((rows, width), dt) for dt in out_dtypes]
        + [jax.ShapeDtypeStruct((rows, n_gate_tiles * IN_TILE), BF16)],
        scratch_shapes=[pltpu.VMEM((tm, d), BF16)],
        compiler_params=pltpu.CompilerParams(
            dimension_semantics=("parallel", "arbitrary"), vmem_limit_bytes=_vmem_limit(est)),
        name="inproj",
    )(x, g.reshape(1, d), sc, sh, w_in)


A_BLOCK = 256


A_HEADS = 2
A_VARIANTS = 3
TOEPLITZ_ROW = 1024


def _toeplitz_row(rel_bias, offset, n_cols):
    m = jnp.arange(TOEPLITZ_ROW)
    m = jnp.where(m < n_cols, m, m - TOEPLITZ_ROW)
    idx = jnp.clip(offset - m, -REL_CLIP, REL_CLIP) + REL_CLIP
    return rel_bias.astype(F32)[:, None, idx]


def _toeplitz_bias(row, n_rows):
    return pltpu.roll(jnp.broadcast_to(row, (n_rows, TOEPLITZ_ROW)), 0, 1, stride=1, stride_axis=0)


def _attn_a_prompt_kernel(g_ref, q_ref, k0_ref, k1_ref, k2_ref, v0_ref, v1_ref, v2_ref, o_ref, bias_ref):
    qi = pl.program_id(2)
    n_keys = 3 * A_BLOCK

    @pl.when(qi == 0)
    def _():
        qc = lax.broadcasted_iota(jnp.int32, (A_BLOCK, n_keys), 0) // CHUNK
        col = lax.broadcasted_iota(jnp.int32, (A_BLOCK, n_keys), 1)
        kc = col // CHUNK - BAND // CHUNK
        in_band = jnp.logical_and(kc <= qc, kc >= qc - BAND_CHUNKS)
        for hh in range(A_HEADS):
            rel = _toeplitz_bias(g_ref[hh], A_BLOCK)[:, :n_keys] * LOG2_E
            full = jnp.where(in_band, rel, MASK_VALUE)
            for v in range(A_VARIANTS):
                bias_ref[v, hh] = jnp.where(col >= (A_VARIANTS - 1 - v) * A_BLOCK, full, MASK_VALUE)

    variant = jnp.minimum(qi, A_VARIANTS - 1)
    for hh in range(A_HEADS):
        cols = slice(hh * HEAD_DIM, (hh + 1) * HEAD_DIM)
        q = q_ref[:, cols]
        scores = [
            _dot_nt(q, k_ref[:, cols].astype(BF16))
            + bias_ref[variant, hh, :, jb * A_BLOCK:(jb + 1) * A_BLOCK]
            for jb, k_ref in enumerate((k0_ref, k1_ref, k2_ref))]
        m = functools.reduce(jnp.maximum, [jnp.max(s, axis=-1, keepdims=True) for s in scores])
        es = [jnp.exp2(s - m) for s in scores]
        l = functools.reduce(jnp.add, [jnp.sum(e, axis=-1, keepdims=True) for e in es])
        acc = functools.reduce(jnp.add, [
            _dot(e.astype(BF16), v_ref[:, cols].astype(BF16))
            for e, v_ref in zip(es, (v0_ref, v1_ref, v2_ref))])
        o_ref[:, cols] = (acc / l).astype(o_ref.dtype)


def _attn_a_prompt(q, k, v, rel_bias, batch, seq):
    rows, width = q.shape
    heads = width // HEAD_DIM
    nq = seq // A_BLOCK
    assert BAND == 2 * A_BLOCK and seq % A_BLOCK == 0 and heads % A_HEADS == 0
    assert 4 * A_BLOCK - 1 <= TOEPLITZ_ROW
    g = _toeplitz_row(rel_bias, BAND, 3 * A_BLOCK)
    blk = (A_BLOCK, A_HEADS * HEAD_DIM)

    def kv_spec(jb):
        return pl.BlockSpec(blk, lambda b, h, i: (b * nq + jnp.maximum(i + (jb - 2), 0), h))

    est = (2 * A_BLOCK * A_HEADS * HEAD_DIM * (2 + 6 * 4 + 2)
           + A_HEADS * (A_VARIANTS * A_BLOCK * 3 * A_BLOCK * 4 + 8 * A_BLOCK * 3 * A_BLOCK * 4))
    return pl.pallas_call(
        _attn_a_prompt_kernel,
        grid=(batch, heads // A_HEADS, nq),
        in_specs=[pl.BlockSpec((A_HEADS, 1, TOEPLITZ_ROW), lambda b, h, i: (h, 0, 0)),
                  pl.BlockSpec(blk, lambda b, h, i: (b * nq + i, h)),
                  kv_spec(0), kv_spec(1), kv_spec(2), kv_spec(0), kv_spec(1), kv_spec(2)],
        out_specs=pl.BlockSpec(blk, lambda b, h, i: (b * nq + i, h)),
        out_shape=jax.ShapeDtypeStruct((rows, width), BF16),
        scratch_shapes=[pltpu.VMEM((A_VARIANTS, A_HEADS, A_BLOCK, 3 * A_BLOCK), F32)],
        compiler_params=pltpu.CompilerParams(
            dimension_semantics=("parallel", "parallel", "arbitrary"),
            vmem_limit_bytes=_vmem_limit(est)),
        name="attn_a_prompt",
    )(g, q, k, k, k, v, v, v)


def _attn_a_sample_kernel(g_ref, q_ref, kc_ref, vc_ref, kn_ref, vn_ref, o_ref):
    q = q_ref[...]
    n = q.shape[0]
    lc = kc_ref.shape[1]
    bias = _toeplitz_bias(g_ref[0], n) * LOG2_E
    sc = _dot_nt(q, kc_ref[0].astype(BF16)) + bias[:, :lc]
    sn = _dot_nt(q, kn_ref[...].astype(BF16)) + bias[:, lc:lc + n]
    m = jnp.maximum(jnp.max(sc, axis=-1, keepdims=True), jnp.max(sn, axis=-1, keepdims=True))
    ec = jnp.exp2(sc - m)
    en = jnp.exp2(sn - m)
    l = jnp.sum(ec, axis=-1, keepdims=True) + jnp.sum(en, axis=-1, keepdims=True)
    acc = _dot(ec.astype(BF16), vc_ref[0].astype(BF16)) + _dot(en.astype(BF16), vn_ref[...].astype(BF16))
    o_ref[...] = (acc / l).astype(o_ref.dtype)


def _attn_a_sample(q, k, v, cache_k, cache_v, rel_bias, batch, n, past_len):
    rows, width = q.shape
    heads = width // HEAD_DIM
    lc = cache_k.shape[1]
    assert past_len >= lc and lc + 2 * n - 1 <= TOEPLITZ_ROW and lc % V7X_LANES == 0
    g = _toeplitz_row(rel_bias, lc, lc + n)
    blk = (n, HEAD_DIM)
    cblk = (1, lc, HEAD_DIM)
    est = 2 * (2 * lc * HEAD_DIM * 4 + n * lc * 4) + 8 * n * lc * 4
    return pl.pallas_call(
        _attn_a_sample_kernel,
        grid=(batch, heads),
        in_specs=[pl.BlockSpec((1, 1, TOEPLITZ_ROW), lambda b, h: (h, 0, 0)),
                  pl.BlockSpec(blk, lambda b, h: (b, h)),
                  pl.BlockSpec(cblk, lambda b, h: (b, 0, h)),
                  pl.BlockSpec(cblk, lambda b, h: (b, 0, h)),
                  pl.BlockSpec(blk, lambda b, h: (b, h)),
                  pl.BlockSpec(blk, lambda b, h: (b, h))],
        out_specs=pl.BlockSpec(blk, lambda b, h: (b, h)),
        out_shape=jax.ShapeDtypeStruct((rows, width), BF16),
        compiler_params=pltpu.CompilerParams(
            dimension_semantics=("parallel", "parallel"), vmem_limit_bytes=_vmem_limit(est)),
        name="attn_a_sample",
    )(g, q, cache_k, cache_v, k, v)


B_BLOCK = 256
B_HEADS = 2


def _later_matrix(n):
    j = jnp.arange(n)[:, None]
    s = jnp.arange(n)[None, :]
    return (j > s).astype(BF16)


def _strict_causal_mask(n):
    t = jnp.arange(n)[:, None]
    s = jnp.arange(n)[None, :]
    return jnp.where(s < t, 0.0, MASK_VALUE).astype(F32)


def _stick_terms(q, k, later_mat, z_mask=None, valid=None):
    z = _dot_nt(q, k)
    if z_mask is not None:
        z = z + z_mask
    if valid is not None:
        z = jnp.where(valid, z, MASK_VALUE)
    log_1m = -(jnp.maximum(z, 0.0) + jnp.log(1.0 + jnp.exp(-jnp.abs(z))))
    hi = log_1m.astype(BF16)
    lo = (log_1m - hi.astype(F32)).astype(BF16)
    later = _dot(hi, later_mat) + _dot(lo, later_mat)
    return z + log_1m + later, jnp.sum(log_1m, axis=-1, keepdims=True)


def _stick_sweep_past(qs, k_ref, v_ref, n_blocks, tk, cs, accs, later_mat):
    heads = len(qs)

    def cond(carry):
        j, cs, _ = carry
        return jnp.logical_and(j >= 0, jnp.max(functools.reduce(jnp.maximum, cs)) > EXP_ZERO_BELOW)

    def body(carry):
        j, cs, accs = carry
        rows = pl.ds(pl.multiple_of(j * tk, tk), tk)
        new_cs, new_accs = [], []
        for hh in range(heads):
            cols = slice(hh * HEAD_DIM, (hh + 1) * HEAD_DIM)
            logit, row_sum = _stick_terms(qs[hh], k_ref[rows, cols].astype(BF16), later_mat)
            weights = jnp.exp(logit + cs[hh]).astype(BF16)
            new_accs.append(accs[hh] + _dot(weights, v_ref[rows, cols].astype(BF16)))
            new_cs.append(cs[hh] + row_sum)
        return j - 1, tuple(new_cs), tuple(new_accs)

    _, _, accs = lax.while_loop(cond, body, (n_blocks - 1, tuple(cs), tuple(accs)))
    return accs


def _stick_heads(q_ref, k0, v0, k1, v1, later0, mask0, later_mat, valid1, k_ref, v_ref, n_blocks, o_ref):
    heads = q_ref.shape[1] // HEAD_DIM
    qs, cs, accs = [], [], []
    for hh in range(heads):
        cols = slice(hh * HEAD_DIM, (hh + 1) * HEAD_DIM)
        q = q_ref[:, cols]
        logit0, sum0 = _stick_terms(q, k0(cols).astype(BF16), later0, z_mask=mask0)
        logit1, sum1 = _stick_terms(q, k1(cols).astype(BF16), later_mat, valid=valid1)
        accs.append(_dot(jnp.exp(logit0).astype(BF16), v0(cols).astype(BF16))
                    + _dot(jnp.exp(logit1 + sum0).astype(BF16), v1(cols).astype(BF16)))
        cs.append(sum0 + sum1)
        qs.append(q)
    accs = _stick_sweep_past(qs, k_ref, v_ref, n_blocks, B_BLOCK, cs, accs, later_mat)
    for hh in range(heads):
        o_ref[:, hh * HEAD_DIM:(hh + 1) * HEAD_DIM] = accs[hh].astype(o_ref.dtype)


def _attn_b_prompt_kernel(q_ref, k_ref, v_ref, u_ref, mask_ref, o_ref):
    qi = pl.program_id(2)
    k_ref = k_ref.at[0]
    v_ref = v_ref.at[0]
    later_mat = u_ref[...]
    rows0 = pl.ds(pl.multiple_of(qi * B_BLOCK, B_BLOCK), B_BLOCK)
    rows1 = pl.ds(pl.multiple_of(jnp.maximum(qi - 1, 0) * B_BLOCK, B_BLOCK), B_BLOCK)
    _stick_heads(q_ref,
                 lambda cols: k_ref[rows0, cols], lambda cols: v_ref[rows0, cols],
                 lambda cols: k_ref[rows1, cols], lambda cols: v_ref[rows1, cols],
                 later_mat, mask_ref[...], later_mat, qi > 0, k_ref, v_ref, qi - 1, o_ref)


def _attn_b_prompt(q, k, v, batch, seq):
    rows, width = q.shape
    heads = width // HEAD_DIM
    assert heads % B_HEADS == 0
    nq = seq // B_BLOCK
    k3 = k.reshape(batch, seq, width)
    v3 = v.reshape(batch, seq, width)
    lanes = B_HEADS * HEAD_DIM
    blk = (B_BLOCK, lanes)
    kv_spec = pl.BlockSpec((1, seq, lanes), lambda b, h, i: (b, 0, h))
    const = pl.BlockSpec((B_BLOCK, B_BLOCK), lambda b, h, i: (0, 0))
    est = 2 * 2 * seq * lanes * 4 + B_HEADS * 40 * B_BLOCK * B_BLOCK * 4
    return pl.pallas_call(
        _attn_b_prompt_kernel,
        grid=(batch, heads // B_HEADS, nq),
        in_specs=[pl.BlockSpec(blk, lambda b, h, i: (b * nq + i, h)), kv_spec, kv_spec, const, const],
        out_specs=pl.BlockSpec(blk, lambda b, h, i: (b * nq + i, h)),
        out_shape=jax.ShapeDtypeStruct((rows, width), BF16),
        compiler_params=pltpu.CompilerParams(
            dimension_semantics=("parallel", "parallel", "parallel"),
            vmem_limit_bytes=_vmem_limit(est)),
        name="attn_b_prompt",
    )(q, k3, v3, _later_matrix(B_BLOCK), _strict_causal_mask(B_BLOCK))


def _attn_b_sample_kernel(q_ref, kc_ref, vc_ref, kn_ref, vn_ref, un_ref, mask_ref, u_ref, o_ref,
                          *, n_blocks):
    kc_ref = kc_ref.at[0]
    vc_ref = vc_ref.at[0]
    rows1 = pl.ds((n_blocks - 1) * B_BLOCK, B_BLOCK)
    _stick_heads(q_ref,
                 lambda cols: kn_ref[:, cols], lambda cols: vn_ref[:, cols],
                 lambda cols: kc_ref[rows1, cols], lambda cols: vc_ref[rows1, cols],
                 un_ref[...], mask_ref[...], u_ref[...], None, kc_ref, vc_ref, n_blocks - 1, o_ref)


def _attn_b_sample(q, k, v, cache_k, cache_v, batch, n):
    rows, width = q.shape
    heads = width // HEAD_DIM
    lc = cache_k.shape[1]
    assert lc % B_BLOCK == 0 and lc >= B_BLOCK and heads % B_HEADS == 0
    lanes = B_HEADS * HEAD_DIM
    blk = (n, lanes)
    cblk = (1, lc, lanes)
    const = lambda shape: pl.BlockSpec(shape, lambda b, h: (0, 0))
    est = 2 * 2 * lc * lanes * 4 + B_HEADS * 40 * n * B_BLOCK * 4 + 8 * B_BLOCK * B_BLOCK
    return pl.pallas_call(
        functools.partial(_attn_b_sample_kernel, n_blocks=lc // B_BLOCK),
        grid=(batch, heads // B_HEADS),
        in_specs=[pl.BlockSpec(blk, lambda b, h: (b, h)),
                  pl.BlockSpec(cblk, lambda b, h: (b, 0, h)),
                  pl.BlockSpec(cblk, lambda b, h: (b, 0, h)),
                  pl.BlockSpec(blk, lambda b, h: (b, h)),
                  pl.BlockSpec(blk, lambda b, h: (b, h)),
                  const((n, n)), const((n, n)), const((B_BLOCK, B_BLOCK))],
        out_specs=pl.BlockSpec(blk, lambda b, h: (b, h)),
        out_shape=jax.ShapeDtypeStruct((rows, width), BF16),
        compiler_params=pltpu.CompilerParams(
            dimension_semantics=("parallel", "parallel"), vmem_limit_bytes=_vmem_limit(est)),
        name="attn_b_sample",
    )(q, cache_k, cache_v, k, v, _later_matrix(n), _strict_causal_mask(n), _later_matrix(B_BLOCK))


MIX_COLS = 512


def _mix_out_kernel(x_ref, oa_ref, ob_ref, gates_ref, gt_ref, wa_ref, wb_ref, wo_ref, o_ref, y_ref,
                    *, groups, ts):
    d = o_ref.shape[-1]
    oa = oa_ref[...]
    ob = ob_ref[...]
    for c0 in range(0, d, MIX_COLS):
        cols = slice(c0, c0 + MIX_COLS)
        ya = _dot(oa, wa_ref[:, cols])
        yb = _dot(ob, wb_ref[:, cols])
        ga = gates_ref[:, cols].astype(F32)
        gb = gates_ref[:, d + c0:d + c0 + MIX_COLS].astype(F32)
        y_ref[:, cols] = (_sigmoid(ga) * ya + _sigmoid(gb) * yb).astype(BF16)
    y = y_ref[...]
    for c0 in range(0, d, MIX_COLS):
        cols = slice(c0, c0 + MIX_COLS)
        merged = _dot(y, wo_ref[:, cols])
        for gi in range(groups):
            rows = slice(gi * ts, (gi + 1) * ts)
            o_ref[gi, :, cols] = x_ref[gi, :, cols] + gt_ref[gi, :, cols] * merged[rows]


def _mix_out(x, oa, ob, gates, gt, wa, wb, wo, tile_rows):
    batch, seq, d = x.shape
    width = oa.shape[1]
    groups, ts, n_tiles, x_map, b_map = _row_tiling(batch, seq, tile_rows)
    tm = groups * ts
    const = lambda i: (0, 0)
    single = pl.Buffered(1)
    est = (2 * 2 * tm * d * 4 + 2 * 2 * tm * width * 2 + 2 * tm * 2 * d * 2 + tm * d * 2
           + (2 * width * d + d * d) * 2 + 6 * tm * MIX_COLS * 4)
    return pl.pallas_call(
        functools.partial(_mix_out_kernel, groups=groups, ts=ts),
        grid=(n_tiles,),
        in_specs=[
            pl.BlockSpec((groups, ts, d), x_map),
            pl.BlockSpec((tm, width), lambda i: (i, 0)),
            pl.BlockSpec((tm, width), lambda i: (i, 0)),
            pl.BlockSpec((tm, 2 * d), lambda i: (i, 0)),
            pl.BlockSpec((groups, 1, d), b_map),
            pl.BlockSpec((width, d), const, pipeline_mode=single),
            pl.BlockSpec((width, d), const, pipeline_mode=single),
            pl.BlockSpec((d, d), const, pipeline_mode=single),
        ],
        out_specs=pl.BlockSpec((groups, ts, d), x_map),
        out_shape=jax.ShapeDtypeStruct((batch, seq, d), F32),
        scratch_shapes=[pltpu.VMEM((tm, d), BF16)],
        compiler_params=pltpu.CompilerParams(
            dimension_semantics=("parallel",), vmem_limit_bytes=_vmem_limit(est)),
        name="mix_out",
    )(x, oa, ob, gates, gt, wa, wb, wo)


FF_TILE = 512


def _ffn_kernel(x_ref, g_ref, sc_ref, sh_ref, gt_ref, gf_ref, wg_ref, wu_ref, wd_ref, o_ref,
                h_ref, *, groups, ts):
    f = pl.program_id(1)

    @pl.when(f == 0)
    def _():
        _norm_mod_rows(x_ref, g_ref, sc_ref, sh_ref, h_ref, groups, ts)
        o_ref[...] = jnp.zeros_like(o_ref)

    h = h_ref[...]
    gate = _dot(h, wg_ref[...])
    up = _dot(h, wu_ref[...])
    hidden = (gate * _sigmoid(gate) * up).astype(BF16)
    if groups == 1:
        o_ref[0] += _dot(hidden, wd_ref[...])
    else:
        part = _dot(hidden, wd_ref[...])
        for gi in range(groups):
            o_ref[gi] += part[gi * ts:(gi + 1) * ts]

    @pl.when(f == pl.num_programs(1) - 1)
    def _():
        gf = gf_ref[...]
        for gi in range(groups):
            gt = gt_ref[gi]
            for r0 in range(0, ts, NORM_ROWS):
                rows = slice(r0, r0 + NORM_ROWS)
                x2 = x_ref[gi, rows, :] + gt * o_ref[gi, rows, :]
                ms = jnp.mean(x2 * x2, axis=-1, keepdims=True)
                o_ref[gi, rows, :] = x2 * lax.rsqrt(ms + EPS) * gf


def _ffn(x, g, sc, sh, gt, g_final, w_gate_up, w_down, tile_rows):
    batch, seq, d = x.shape
    d_ff = w_down.shape[0]
    nf = d_ff // FF_TILE
    groups, ts, n_tiles, x_map, b_map = _row_tiling(batch, seq, tile_rows)
    tm = groups * ts
    xm = lambda i, f: x_map(i)
    bm = lambda i, f: b_map(i)
    vec = pl.BlockSpec((1, d), lambda i, f: (0, 0))
    mod = pl.BlockSpec((groups, 1, d), bm)
    est = 3 * tm * d * 4 + tm * d * 2 + 2 * 3 * d * FF_TILE * 2 + 5 * tm * FF_TILE * 4
    return pl.pallas_call(
        functools.partial(_ffn_kernel, groups=groups, ts=ts),
        grid=(n_tiles, nf),
        in_specs=[
            pl.BlockSpec((groups, ts, d), xm, pipeline_mode=pl.Buffered(1)), vec, mod, mod, mod, vec,
            pl.BlockSpec((d, FF_TILE), lambda i, f: (0, f)),
            pl.BlockSpec((d, FF_TILE), lambda i, f: (0, nf + f)),
            pl.BlockSpec((FF_TILE, d), lambda i, f: (f, 0)),
        ],
        out_specs=pl.BlockSpec((groups, ts, d), xm),
        out_shape=jax.ShapeDtypeStruct((batch, seq, d), F32),
        scratch_shapes=[pltpu.VMEM((tm, d), BF16)],
        compiler_params=pltpu.CompilerParams(
            dimension_semantics=("parallel", "arbitrary"), vmem_limit_bytes=_vmem_limit(est)),
        name="ffn",
    )(x, g.reshape(1, d), sc, sh, gt, g_final.reshape(1, d), w_gate_up, w_gate_up, w_down)


def _layer(x, mods, w, attn_a, attn_b, g_final, in_rows, mix_rows, ffn_rows):
    batch, seq, d = x.shape
    sh1, sc1, gt1, sh2, sc2, gt2 = mods
    width = w["w_a_out"].shape[0]
    qa, ka, va, qb, kb, vb, gates = _inproj(x, w["g_mix"], sc1, sh1, w["w_in"], in_rows, width)
    oa = attn_a(qa, ka, va)
    ob = attn_b(qb, kb, vb)
    x1 = _mix_out(x, oa, ob, gates, gt1, w["w_a_out"], w["w_b_out"], w["w_o"], mix_rows)
    y = _ffn(x1, w["g_ffn"], sc2, sh2, gt2, g_final, w["w_gate_up"], w["w_down"], ffn_rows)
    heads_of = lambda t: t.reshape(batch, seq, t.shape[1] // HEAD_DIM, HEAD_DIM)
    return y, heads_of(ka), heads_of(va), heads_of(kb), heads_of(vb)


def kernel(x_prompt, x_sample, c_prompt, c_sample, cache_a_k, cache_a_v, cache_b_k, cache_b_v,
           w_ada, b_ada, g_mix, w_in, rel_bias, w_a_out, w_b_out, w_o, g_ffn, w_gate_up, w_down,
           g_final):
    depth = w_in.shape[0]
    assert depth == 1
    batch, seq, d = x_prompt.shape
    dec_batch, dec_seq, _ = x_sample.shape
    past_len = cache_b_k.shape[2]
    l = 0

    n_c = batch + dec_batch
    c_rows = -(-n_c // 8) * 8
    c_all = jnp.concatenate([c_prompt, c_sample, jnp.zeros((c_rows - n_c, d), F32)], axis=0)
    mod_all = _ada(c_all, w_ada[l], b_ada[l])

    def mods_of(rows):
        return tuple(m[:, None, :] for m in jnp.split(rows, 6, axis=-1))

    mods_p = mods_of(mod_all[:batch])
    mods_s = mods_of(mod_all[batch:n_c])

    w = {
        "g_mix": g_mix[l], "g_ffn": g_ffn[l],
        "w_in": w_in[l].astype(BF16), "w_a_out": w_a_out[l].astype(BF16),
        "w_b_out": w_b_out[l].astype(BF16), "w_o": w_o[l].astype(BF16),
        "w_gate_up": w_gate_up[l].astype(BF16), "w_down": w_down[l].astype(BF16),
    }

    yp, ka, va, kb, vb = _layer(
        x_prompt, mods_p, w,
        functools.partial(_attn_a_prompt, rel_bias=rel_bias[l], batch=batch, seq=seq),
        functools.partial(_attn_b_prompt, batch=batch, seq=seq),
        g_final, in_rows=1024, mix_rows=256, ffn_rows=1024)
    keep = min(BAND, seq)
    outs_p = (ka[:, -keep:], va[:, -keep:], kb, vb)

    la = cache_a_k.shape[2]
    cak = cache_a_k[l].reshape(dec_batch, la, -1)
    cav = cache_a_v[l].reshape(dec_batch, la, -1)
    cbk = cache_b_k[l].reshape(dec_batch, past_len, -1)
    cbv = cache_b_v[l].reshape(dec_batch, past_len, -1)
    ys, ka_s, va_s, kb_s, vb_s = _layer(
        x_sample, mods_s, w,
        functools.partial(_attn_a_sample, cache_k=cak, cache_v=cav, rel_bias=rel_bias[l],
                          batch=dec_batch, n=dec_seq, past_len=past_len),
        functools.partial(_attn_b_sample, cache_k=cbk, cache_v=cbv, batch=dec_batch, n=dec_seq),
        g_final, in_rows=dec_batch * dec_seq, mix_rows=dec_batch * dec_seq,
        ffn_rows=dec_batch * dec_seq)
    outs_s = (ka_s, va_s, kb_s, vb_s)

    return (yp, ys) + tuple(t[None] for t in outs_p) + tuple(t[None] for t in outs_s)
```

```python
import functools

import jax
import jax.numpy as jnp
from jax import lax
from jax.experimental import pallas as pl
from jax.experimental.pallas import tpu as pltpu

F32 = jnp.float32
BF16 = jnp.bfloat16

HEAD_DIM = 128
CHUNK = 64
BAND_CHUNKS = 8
BAND = BAND_CHUNKS * CHUNK
REL_CLIP = 128
EPS = 1e-6
ATTN_SCALE = HEAD_DIM ** -0.5
LOG2_E = 1.4426950408889634
MASK_VALUE = -1e30

V7X_VMEM_BYTES = 64 * 1024 * 1024
V7X_LANES = 128
V7X_BF16_SUBLANES = 16

EXP_ZERO_BELOW = -104.0

NORM_ROWS = 16
NORM_UNROLL = 8


def _norm_unroll(n_rows):
    steps = n_rows // NORM_ROWS
    return NORM_UNROLL if steps % NORM_UNROLL == 0 else 1


def _vmem_limit(estimate_bytes):
    return int(min(V7X_VMEM_BYTES - (4 << 20), max(estimate_bytes * 5 // 4, 16 << 20)))


def _dot(a, b):
    return jnp.dot(a, b, preferred_element_type=F32)


def _dot_nt(a, b):
    return lax.dot_general(a, b, (((1,), (1,)), ((), ())), preferred_element_type=F32)


def _sigmoid(x):
    return 1.0 / (1.0 + jnp.exp(-x))


def _ada_kernel(c_ref, w_ref, b_ref, o_ref):
    c = c_ref[...]
    a = (c * _sigmoid(c)).astype(BF16)
    o_ref[...] = _dot(a, w_ref[...].astype(BF16)) + b_ref[...]


def _ada(c_all, w_ada, b_ada):
    rows, d = c_all.shape
    n = w_ada.shape[1]
    tn = 1024
    est = 2 * d * tn * 4 + d * tn * 2 + 4 * rows * (d + tn) * 4
    return pl.pallas_call(
        _ada_kernel,
        grid=(n // tn,),
        in_specs=[
            pl.BlockSpec((rows, d), lambda j: (0, 0)),
            pl.BlockSpec((d, tn), lambda j: (0, j)),
            pl.BlockSpec((1, tn), lambda j: (0, j)),
        ],
        out_specs=pl.BlockSpec((rows, tn), lambda j: (0, j)),
        out_shape=jax.ShapeDtypeStruct((rows, n), F32),
        compiler_params=pltpu.CompilerParams(
            dimension_semantics=("parallel",), vmem_limit_bytes=_vmem_limit(est)),
        name="ada",
    )(c_all, w_ada, b_ada.reshape(1, n))


def _norm_mod_rows(x_ref, g_ref, sc_ref, sh_ref, h_ref, groups, ts):
    gvec = g_ref[...]
    for gi in range(groups):
        scale = gvec * (1.0 + sc_ref[gi])
        shift = sh_ref[gi]

        def body(r, carry, gi=gi, scale=scale, shift=shift):
            r0 = pl.multiple_of(r * NORM_ROWS, NORM_ROWS)
            x = x_ref[gi, pl.ds(r0, NORM_ROWS), :]
            ms = jnp.mean(x * x, axis=-1, keepdims=True)
            h = x * lax.rsqrt(ms + EPS) * scale + shift
            h_ref[pl.ds(pl.multiple_of(gi * ts + r0, NORM_ROWS), NORM_ROWS), :] = h.astype(BF16)
            return carry

        lax.fori_loop(0, ts // NORM_ROWS, body, 0, unroll=_norm_unroll(ts))


def _row_tiling(batch, seq, tile_rows):
    if seq >= tile_rows:
        assert seq % tile_rows == 0
        groups, ts, ns = 1, tile_rows, seq // tile_rows
    else:
        assert tile_rows % seq == 0 and batch % (tile_rows // seq) == 0
        groups, ts, ns = tile_rows // seq, seq, 1
    n_tiles = batch * seq // (groups * ts)
    if groups == 1:
        x_map = lambda i: (i // ns, i % ns, 0)
        b_map = lambda i: (i // ns, 0, 0)
    else:
        x_map = lambda i: (i, 0, 0)
        b_map = lambda i: (i, 0, 0)
    return groups, ts, n_tiles, x_map, b_map


N_QKV = 6
N_HEADS = 8


def _inproj_kernel(x_ref, g_ref, sc_ref, sh_ref, w_ref,
                   qa_ref, qb_ref, ka_ref, va_ref, kb_ref, vb_ref,
                   ka4_ref, va4_ref, kb4_ref, vb4_ref, gates_ref, h_ref, *, groups, ts):
    j = pl.program_id(1)
    tm = groups * ts

    @pl.when(j == 0)
    def _():
        _norm_mod_rows(x_ref, g_ref, sc_ref, sh_ref, h_ref, groups, ts)

    for idx, o_ref, scale in ((0, qa_ref, ATTN_SCALE * LOG2_E), (3, qb_ref, ATTN_SCALE)):
        @pl.when(j == idx)
        def _(o_ref=o_ref, scale=scale):
            o_ref[...] = (_dot(h_ref[...], w_ref[...]) * scale).astype(o_ref.dtype)

    for idx, o_ref, o4_ref in ((1, ka_ref, ka4_ref), (2, va_ref, va4_ref),
                               (4, kb_ref, kb4_ref), (5, vb_ref, vb4_ref)):
        @pl.when(j == idx)
        def _(o_ref=o_ref, o4_ref=o4_ref):
            p = _dot(h_ref[...], w_ref[...])
            o_ref[...] = p.astype(o_ref.dtype)
            for hd in range(N_HEADS):
                o4_ref[pl.ds(hd, tm, stride=N_HEADS), :] = p[:, hd * HEAD_DIM:(hd + 1) * HEAD_DIM]

    @pl.when(j >= N_QKV)
    def _():
        gates_ref[...] = _dot(h_ref[...], w_ref[...]).astype(gates_ref.dtype)


def _inproj(x, g, sc, sh, w_in, tile_rows, width):
    batch, seq, d = x.shape
    rows = batch * seq
    n_cols = w_in.shape[1]
    assert width == N_HEADS * HEAD_DIM and n_cols % width == 0
    nj = n_cols // width
    n_gate_tiles = nj - N_QKV
    groups, ts, n_tiles, x_map, b_map = _row_tiling(batch, seq, tile_rows)
    tm = groups * ts
    xm = lambda i, j: x_map(i)
    bm = lambda i, j: b_map(i)
    row_spec = pl.BlockSpec((tm, width), lambda i, j: (i, 0))
    flat_spec = pl.BlockSpec((tm * N_HEADS, HEAD_DIM), lambda i, j: (i, 0))
    est = (2 * tm * d * 4 + 2 * d * width * 2 + tm * d * 2
           + 2 * tm * width * (2 * 2 + 4 * 2 + 4 * 4 + 2) + 2 * tm * width * 4)
    bf16_rows = jax.ShapeDtypeStruct((rows, width), BF16)
    f32_flat = jax.ShapeDtypeStruct((rows * N_HEADS, HEAD_DIM), F32)
    return pl.pallas_call(
        functools.partial(_inproj_kernel, groups=groups, ts=ts),
        grid=(n_tiles, nj),
        in_specs=[
            pl.BlockSpec((groups, ts, d), xm),
            pl.BlockSpec((1, d), lambda i, j: (0, 0)),
            pl.BlockSpec((groups, 1, d), bm),
            pl.BlockSpec((groups, 1, d), bm),
            pl.BlockSpec((d, width), lambda i, j: (0, j)),
        ],
        out_specs=[row_spec] * 6 + [flat_spec] * 4 + [
            pl.BlockSpec((tm, width), lambda i, j: (i, jnp.maximum(j - N_QKV, 0)))],
        out_shape=[bf16_rows] * 6 + [f32_flat] * 4
        + [jax.ShapeDtypeStruct((rows, n_gate_tiles * width), BF16)],
        scratch_shapes=[pltpu.VMEM((tm, d), BF16)],
        compiler_params=pltpu.CompilerParams(
            dimension_semantics=("parallel", "arbitrary"), vmem_limit_bytes=_vmem_limit(est)),
        name="inproj",
    )(x, g.reshape(1, d), sc, sh, w_in)


A_BLOCK = 256


A_HEADS = 4
A_VARIANTS = 3
TOEPLITZ_ROW = 1024


def _toeplitz_row(rel_bias, offset, n_cols):
    m = jnp.arange(TOEPLITZ_ROW)
    m = jnp.where(m < n_cols, m, m - TOEPLITZ_ROW)
    idx = jnp.clip(offset - m, -REL_CLIP, REL_CLIP) + REL_CLIP
    return rel_bias.astype(F32)[:, None, idx]


def _toeplitz_bias(row, n_rows):
    return pltpu.roll(jnp.broadcast_to(row, (n_rows, TOEPLITZ_ROW)), 0, 1, stride=1, stride_axis=0)


def _attn_a_prompt_kernel(g_ref, q_ref, k0_ref, k1_ref, k2_ref, v0_ref, v1_ref, v2_ref, o_ref, bias_ref):
    qi = pl.program_id(2)
    n_keys = 3 * A_BLOCK

    @pl.when(qi == 0)
    def _():
        qc = lax.broadcasted_iota(jnp.int32, (A_BLOCK, n_keys), 0) // CHUNK
        col = lax.broadcasted_iota(jnp.int32, (A_BLOCK, n_keys), 1)
        kc = col // CHUNK - BAND // CHUNK
        in_band = jnp.logical_and(kc <= qc, kc >= qc - BAND_CHUNKS)
        for hh in range(A_HEADS):
            rel = _toeplitz_bias(g_ref[hh], A_BLOCK)[:, :n_keys] * LOG2_E
            full = jnp.where(in_band, rel, MASK_VALUE)
            for v in range(A_VARIANTS):
                bias_ref[v, hh] = jnp.where(col >= (A_VARIANTS - 1 - v) * A_BLOCK, full, MASK_VALUE)

    variant = jnp.minimum(qi, A_VARIANTS - 1)
    for hh in range(A_HEADS):
        cols = slice(hh * HEAD_DIM, (hh + 1) * HEAD_DIM)
        q = q_ref[:, cols]
        scores = [
            _dot_nt(q, k_ref[:, cols].astype(BF16))
            + bias_ref[variant, hh, :, jb * A_BLOCK:(jb + 1) * A_BLOCK]
            for jb, k_ref in enumerate((k0_ref, k1_ref, k2_ref))]
        m = jnp.max(functools.reduce(jnp.maximum, scores), axis=-1, keepdims=True)
        es = [jnp.exp2(s - m) for s in scores]
        l = jnp.sum(functools.reduce(jnp.add, es), axis=-1, keepdims=True)
        acc = functools.reduce(jnp.add, [
            _dot(e.astype(BF16), v_ref[:, cols].astype(BF16))
            for e, v_ref in zip(es, (v0_ref, v1_ref, v2_ref))])
        o_ref[:, cols] = (acc / l).astype(o_ref.dtype)


def _attn_a_prompt(q, k, v, rel_bias, batch, seq):
    rows, width = q.shape
    heads = width // HEAD_DIM
    nq = seq // A_BLOCK
    assert BAND == 2 * A_BLOCK and seq % A_BLOCK == 0 and heads % A_HEADS == 0
    assert 4 * A_BLOCK - 1 <= TOEPLITZ_ROW
    g = _toeplitz_row(rel_bias, BAND, 3 * A_BLOCK)
    blk = (A_BLOCK, A_HEADS * HEAD_DIM)

    def kv_spec(jb):
        return pl.BlockSpec(blk, lambda b, h, i: (b * nq + jnp.maximum(i + (jb - 2), 0), h))

    est = (2 * A_BLOCK * A_HEADS * HEAD_DIM * (2 + 6 * 4 + 2)
           + A_HEADS * (A_VARIANTS * A_BLOCK * 3 * A_BLOCK * 4 + 8 * A_BLOCK * 3 * A_BLOCK * 4))
    return pl.pallas_call(
        _attn_a_prompt_kernel,
        grid=(batch, heads // A_HEADS, nq),
        in_specs=[pl.BlockSpec((A_HEADS, 1, TOEPLITZ_ROW), lambda b, h, i: (h, 0, 0)),
                  pl.BlockSpec(blk, lambda b, h, i: (b * nq + i, h)),
                  kv_spec(0), kv_spec(1), kv_spec(2), kv_spec(0), kv_spec(1), kv_spec(2)],
        out_specs=pl.BlockSpec(blk, lambda b, h, i: (b * nq + i, h)),
        out_shape=jax.ShapeDtypeStruct((rows, width), BF16),
        scratch_shapes=[pltpu.VMEM((A_VARIANTS, A_HEADS, A_BLOCK, 3 * A_BLOCK), F32)],
        compiler_params=pltpu.CompilerParams(
            dimension_semantics=("parallel", "parallel", "arbitrary"),
            vmem_limit_bytes=_vmem_limit(est)),
        name="attn_a_prompt",
    )(g, q, k, k, k, v, v, v)


def _attn_a_sample_kernel(g_ref, q_ref, kc_ref, vc_ref, kn_ref, vn_ref, o_ref):
    q = q_ref[...]
    n = q.shape[0]
    lc = kc_ref.shape[1]
    bias = _toeplitz_bias(g_ref[0], n) * LOG2_E
    sc = _dot_nt(q, kc_ref[0].astype(BF16)) + bias[:, :lc]
    sn = _dot_nt(q, kn_ref[...].astype(BF16)) + bias[:, lc:lc + n]
    m = jnp.maximum(jnp.max(sc, axis=-1, keepdims=True), jnp.max(sn, axis=-1, keepdims=True))
    ec = jnp.exp2(sc - m)
    en = jnp.exp2(sn - m)
    l = jnp.sum(ec, axis=-1, keepdims=True) + jnp.sum(en, axis=-1, keepdims=True)
    acc = _dot(ec.astype(BF16), vc_ref[0].astype(BF16)) + _dot(en.astype(BF16), vn_ref[...].astype(BF16))
    o_ref[...] = (acc / l).astype(o_ref.dtype)


def _attn_a_sample(q, k, v, cache_k, cache_v, rel_bias, batch, n, past_len):
    rows, width = q.shape
    heads = width // HEAD_DIM
    lc = cache_k.shape[1]
    assert past_len >= lc and lc + 2 * n - 1 <= TOEPLITZ_ROW and lc % V7X_LANES == 0
    g = _toeplitz_row(rel_bias, lc, lc + n)
    blk = (n, HEAD_DIM)
    cblk = (1, lc, HEAD_DIM)
    est = 2 * (2 * lc * HEAD_DIM * 4 + n * lc * 4) + 8 * n * lc * 4
    return pl.pallas_call(
        _attn_a_sample_kernel,
        grid=(batch, heads),
        in_specs=[pl.BlockSpec((1, 1, TOEPLITZ_ROW), lambda b, h: (h, 0, 0)),
                  pl.BlockSpec(blk, lambda b, h: (b, h)),
                  pl.BlockSpec(cblk, lambda b, h: (b, 0, h)),
                  pl.BlockSpec(cblk, lambda b, h: (b, 0, h)),
                  pl.BlockSpec(blk, lambda b, h: (b, h)),
                  pl.BlockSpec(blk, lambda b, h: (b, h))],
        out_specs=pl.BlockSpec(blk, lambda b, h: (b, h)),
        out_shape=jax.ShapeDtypeStruct((rows, width), BF16),
        compiler_params=pltpu.CompilerParams(
            dimension_semantics=("parallel", "parallel"), vmem_limit_bytes=_vmem_limit(est)),
        name="attn_a_sample",
    )(g, q, cache_k, cache_v, k, v)


B_BLOCK = 256
B_HEADS = 2


def _later_matrix(n):
    j = jnp.arange(n)[:, None]
    s = jnp.arange(n)[None, :]
    return (j > s).astype(BF16)


def _strict_causal_mask(n):
    t = jnp.arange(n)[:, None]
    s = jnp.arange(n)[None, :]
    return jnp.where(s < t, 0.0, MASK_VALUE).astype(F32)


def _stick_terms(q, k, later_mat, z_mask=None, valid=None):
    z = _dot_nt(q, k)
    if z_mask is not None:
        z = z + z_mask
    if valid is not None:
        z = jnp.where(valid, z, MASK_VALUE)
    log_1m = -(jnp.maximum(z, 0.0) + jnp.log(1.0 + jnp.exp(-jnp.abs(z))))
    hi = log_1m.astype(BF16)
    lo = (log_1m - hi.astype(F32)).astype(BF16)
    later = _dot(hi, later_mat) + _dot(lo, later_mat)
    return z + log_1m + later, jnp.sum(log_1m, axis=-1, keepdims=True)


def _stick_sweep_past(qs, k_ref, v_ref, n_blocks, tk, cs, accs, later_mat):
    heads = len(qs)

    def cond(carry):
        j, cs, _ = carry
        return jnp.logical_and(j >= 0, jnp.max(functools.reduce(jnp.maximum, cs)) > EXP_ZERO_BELOW)

    def body(carry):
        j, cs, accs = carry
        rows = pl.ds(pl.multiple_of(j * tk, tk), tk)
        new_cs, new_accs = [], []
        for hh in range(heads):
            cols = slice(hh * HEAD_DIM, (hh + 1) * HEAD_DIM)
            logit, row_sum = _stick_terms(qs[hh], k_ref[rows, cols].astype(BF16), later_mat)
            weights = jnp.exp(logit + cs[hh]).astype(BF16)
            new_accs.append(accs[hh] + _dot(weights, v_ref[rows, cols].astype(BF16)))
            new_cs.append(cs[hh] + row_sum)
        return j - 1, tuple(new_cs), tuple(new_accs)

    _, _, accs = lax.while_loop(cond, body, (n_blocks - 1, tuple(cs), tuple(accs)))
    return accs


def _stick_heads(q_ref, k0, v0, k1, v1, later0, mask0, later_mat, valid1, k_ref, v_ref, n_blocks, o_ref):
    heads = q_ref.shape[1] // HEAD_DIM
    qs, cs, accs = [], [], []
    for hh in range(heads):
        cols = slice(hh * HEAD_DIM, (hh + 1) * HEAD_DIM)
        q = q_ref[:, cols]
        logit0, sum0 = _stick_terms(q, k0(cols).astype(BF16), later0, z_mask=mask0)
        logit1, sum1 = _stick_terms(q, k1(cols).astype(BF16), later_mat, valid=valid1)
        accs.append(_dot(jnp.exp(logit0).astype(BF16), v0(cols).astype(BF16))
                    + _dot(jnp.exp(logit1 + sum0).astype(BF16), v1(cols).astype(BF16)))
        cs.append(sum0 + sum1)
        qs.append(q)
    accs = _stick_sweep_past(qs, k_ref, v_ref, n_blocks, B_BLOCK, cs, accs, later_mat)
    for hh in range(heads):
        o_ref[:, hh * HEAD_DIM:(hh + 1) * HEAD_DIM] = accs[hh].astype(o_ref.dtype)


def _attn_b_prompt_kernel(q_ref, k_ref, v_ref, u_ref, mask_ref, o_ref):
    qi = pl.program_id(2)
    k_ref = k_ref.at[0]
    v_ref = v_ref.at[0]
    later_mat = u_ref[...]
    rows0 = pl.ds(pl.multiple_of(qi * B_BLOCK, B_BLOCK), B_BLOCK)
    rows1 = pl.ds(pl.multiple_of(jnp.maximum(qi - 1, 0) * B_BLOCK, B_BLOCK), B_BLOCK)
    _stick_heads(q_ref,
                 lambda cols: k_ref[rows0, cols], lambda cols: v_ref[rows0, cols],
                 lambda cols: k_ref[rows1, cols], lambda cols: v_ref[rows1, cols],
                 later_mat, mask_ref[...], later_mat, qi > 0, k_ref, v_ref, qi - 1, o_ref)


def _attn_b_prompt(q, k, v, batch, seq):
    rows, width = q.shape
    heads = width // HEAD_DIM
    assert heads % B_HEADS == 0
    nq = seq // B_BLOCK
    k3 = k.reshape(batch, seq, width)
    v3 = v.reshape(batch, seq, width)
    lanes = B_HEADS * HEAD_DIM
    blk = (B_BLOCK, lanes)
    kv_spec = pl.BlockSpec((1, seq, lanes), lambda b, h, i: (b, 0, h))
    const = pl.BlockSpec((B_BLOCK, B_BLOCK), lambda b, h, i: (0, 0))
    est = 2 * 2 * seq * lanes * k.dtype.itemsize + B_HEADS * 40 * B_BLOCK * B_BLOCK * 4
    return pl.pallas_call(
        _attn_b_prompt_kernel,
        grid=(batch, heads // B_HEADS, nq),
        in_specs=[pl.BlockSpec(blk, lambda b, h, i: (b * nq + i, h)), kv_spec, kv_spec, const, const],
        out_specs=pl.BlockSpec(blk, lambda b, h, i: (b * nq + i, h)),
        out_shape=jax.ShapeDtypeStruct((rows, width), BF16),
        compiler_params=pltpu.CompilerParams(
            dimension_semantics=("parallel", "parallel", "parallel"),
            vmem_limit_bytes=_vmem_limit(est)),
        name="attn_b_prompt",
    )(q, k3, v3, _later_matrix(B_BLOCK), _strict_causal_mask(B_BLOCK))


def _attn_b_sample_kernel(q_ref, kc_ref, vc_ref, kn_ref, vn_ref, un_ref, mask_ref, u_ref, o_ref,
                          *, n_blocks):
    kc_ref = kc_ref.at[0]
    vc_ref = vc_ref.at[0]
    rows1 = pl.ds((n_blocks - 1) * B_BLOCK, B_BLOCK)
    _stick_heads(q_ref,
                 lambda cols: kn_ref[:, cols], lambda cols: vn_ref[:, cols],
                 lambda cols: kc_ref[rows1, cols], lambda cols: vc_ref[rows1, cols],
                 un_ref[...], mask_ref[...], u_ref[...], None, kc_ref, vc_ref, n_blocks - 1, o_ref)


def _attn_b_sample(q, k, v, cache_k, cache_v, batch, n):
    rows, width = q.shape
    heads = width // HEAD_DIM
    lc = cache_k.shape[1]
    assert lc % B_BLOCK == 0 and lc >= B_BLOCK and heads % B_HEADS == 0
    lanes = B_HEADS * HEAD_DIM
    blk = (n, lanes)
    cblk = (1, lc, lanes)
    const = lambda shape: pl.BlockSpec(shape, lambda b, h: (0, 0))
    est = 2 * 2 * lc * lanes * 4 + B_HEADS * 40 * n * B_BLOCK * 4 + 8 * B_BLOCK * B_BLOCK
    return pl.pallas_call(
        functools.partial(_attn_b_sample_kernel, n_blocks=lc // B_BLOCK),
        grid=(batch, heads // B_HEADS),
        in_specs=[pl.BlockSpec(blk, lambda b, h: (b, h)),
                  pl.BlockSpec(cblk, lambda b, h: (b, 0, h)),
                  pl.BlockSpec(cblk, lambda b, h: (b, 0, h)),
                  pl.BlockSpec(blk, lambda b, h: (b, h)),
                  pl.BlockSpec(blk, lambda b, h: (b, h)),
                  const((n, n)), const((n, n)), const((B_BLOCK, B_BLOCK))],
        out_specs=pl.BlockSpec(blk, lambda b, h: (b, h)),
        out_shape=jax.ShapeDtypeStruct((rows, width), BF16),
        compiler_params=pltpu.CompilerParams(
            dimension_semantics=("parallel", "parallel"), vmem_limit_bytes=_vmem_limit(est)),
        name="attn_b_sample",
    )(q, cache_k, cache_v, k, v, _later_matrix(n), _strict_causal_mask(n), _later_matrix(B_BLOCK))


MIX_COLS = 512


def _mix_out_kernel(x_ref, oa_ref, ob_ref, gates_ref, gt_ref, wa_ref, wb_ref, wo_ref, o_ref, y_ref,
                    *, groups, ts):
    d = o_ref.shape[-1]
    oa = oa_ref[...]
    ob = ob_ref[...]
    for c0 in range(0, d, MIX_COLS):
        cols = slice(c0, c0 + MIX_COLS)
        ya = _dot(oa, wa_ref[:, cols])
        yb = _dot(ob, wb_ref[:, cols])
        ga = gates_ref[:, cols].astype(F32)
        gb = gates_ref[:, d + c0:d + c0 + MIX_COLS].astype(F32)
        y_ref[:, cols] = (_sigmoid(ga) * ya + _sigmoid(gb) * yb).astype(BF16)
    y = y_ref[...]
    for c0 in range(0, d, MIX_COLS):
        cols = slice(c0, c0 + MIX_COLS)
        merged = _dot(y, wo_ref[:, cols])
        for gi in range(groups):
            rows = slice(gi * ts, (gi + 1) * ts)
            o_ref[gi, :, cols] = x_ref[gi, :, cols] + gt_ref[gi, :, cols] * merged[rows]


def _mix_out(x, oa, ob, gates, gt, wa, wb, wo, tile_rows):
    batch, seq, d = x.shape
    width = oa.shape[1]
    groups, ts, n_tiles, x_map, b_map = _row_tiling(batch, seq, tile_rows)
    tm = groups * ts
    const = lambda i: (0, 0)
    single = pl.Buffered(1)
    est = (2 * 2 * tm * d * 4 + 2 * 2 * tm * width * 2 + 2 * tm * 2 * d * 2 + tm * d * 2
           + (2 * width * d + d * d) * 2 + 6 * tm * MIX_COLS * 4)
    return pl.pallas_call(
        functools.partial(_mix_out_kernel, groups=groups, ts=ts),
        grid=(n_tiles,),
        in_specs=[
            pl.BlockSpec((groups, ts, d), x_map),
            pl.BlockSpec((tm, width), lambda i: (i, 0)),
            pl.BlockSpec((tm, width), lambda i: (i, 0)),
            pl.BlockSpec((tm, 2 * d), lambda i: (i, 0)),
            pl.BlockSpec((groups, 1, d), b_map),
            pl.BlockSpec((width, d), const, pipeline_mode=single),
            pl.BlockSpec((width, d), const, pipeline_mode=single),
            pl.BlockSpec((d, d), const, pipeline_mode=single),
        ],
        out_specs=pl.BlockSpec((groups, ts, d), x_map),
        out_shape=jax.ShapeDtypeStruct((batch, seq, d), F32),
        scratch_shapes=[pltpu.VMEM((tm, d), BF16)],
        compiler_params=pltpu.CompilerParams(
            dimension_semantics=("parallel",), vmem_limit_bytes=_vmem_limit(est)),
        name="mix_out",
    )(x, oa, ob, gates, gt, wa, wb, wo)


FF_TILE = 512


def _ffn_kernel(x_ref, g_ref, sc_ref, sh_ref, gt_ref, gf_ref, wg_ref, wu_ref, wd_ref, o_ref,
                h_ref, *, groups, ts):
    f = pl.program_id(1)

    @pl.when(f == 0)
    def _():
        _norm_mod_rows(x_ref, g_ref, sc_ref, sh_ref, h_ref, groups, ts)
        o_ref[...] = jnp.zeros_like(o_ref)

    h = h_ref[...]
    gate = _dot(h, wg_ref[...])
    up = _dot(h, wu_ref[...])
    hidden = (gate * _sigmoid(gate) * up).astype(BF16)
    if groups == 1:
        o_ref[0] += _dot(hidden, wd_ref[...])
    else:
        part = _dot(hidden, wd_ref[...])
        for gi in range(groups):
            o_ref[gi] += part[gi * ts:(gi + 1) * ts]

    @pl.when(f == pl.num_programs(1) - 1)
    def _():
        gf = gf_ref[...]
        for gi in range(groups):
            gt = gt_ref[gi]
            for r0 in range(0, ts, NORM_ROWS):
                rows = slice(r0, r0 + NORM_ROWS)
                x2 = x_ref[gi, rows, :] + gt * o_ref[gi, rows, :]
                ms = jnp.mean(x2 * x2, axis=-1, keepdims=True)
                o_ref[gi, rows, :] = x2 * lax.rsqrt(ms + EPS) * gf


def _ffn(x, g, sc, sh, gt, g_final, w_gate_up, w_down, tile_rows):
    batch, seq, d = x.shape
    d_ff = w_down.shape[0]
    nf = d_ff // FF_TILE
    groups, ts, n_tiles, x_map, b_map = _row_tiling(batch, seq, tile_rows)
    tm = groups * ts
    xm = lambda i, f: x_map(i)
    bm = lambda i, f: b_map(i)
    vec = pl.BlockSpec((1, d), lambda i, f: (0, 0))
    mod = pl.BlockSpec((groups, 1, d), bm)
    est = 3 * tm * d * 4 + tm * d * 2 + 2 * 3 * d * FF_TILE * 2 + 5 * tm * FF_TILE * 4
    return pl.pallas_call(
        functools.partial(_ffn_kernel, groups=groups, ts=ts),
        grid=(n_tiles, nf),
        in_specs=[
            pl.BlockSpec((groups, ts, d), xm, pipeline_mode=pl.Buffered(1)), vec, mod, mod, mod, vec,
            pl.BlockSpec((d, FF_TILE), lambda i, f: (0, f)),
            pl.BlockSpec((d, FF_TILE), lambda i, f: (0, nf + f)),
            pl.BlockSpec((FF_TILE, d), lambda i, f: (f, 0)),
        ],
        out_specs=pl.BlockSpec((groups, ts, d), xm),
        out_shape=jax.ShapeDtypeStruct((batch, seq, d), F32),
        scratch_shapes=[pltpu.VMEM((tm, d), BF16)],
        compiler_params=pltpu.CompilerParams(
            dimension_semantics=("parallel", "arbitrary"), vmem_limit_bytes=_vmem_limit(est)),
        name="ffn",
    )(x, g.reshape(1, d), sc, sh, gt, g_final.reshape(1, d), w_gate_up, w_gate_up, w_down)


def _layer(x, mods, w, attn_a, attn_b, g_final, in_rows, mix_rows, ffn_rows):
    batch, seq, d = x.shape
    sh1, sc1, gt1, sh2, sc2, gt2 = mods
    width = w["w_a_out"].shape[0]
    qa, qb, ka, va, kb, vb, ka4, va4, kb4, vb4, gates = _inproj(
        x, w["g_mix"], sc1, sh1, w["w_in"], in_rows, width)
    oa = attn_a(qa, ka, va)
    ob = attn_b(qb, kb, vb)
    x1 = _mix_out(x, oa, ob, gates, gt1, w["w_a_out"], w["w_b_out"], w["w_o"], mix_rows)
    y = _ffn(x1, w["g_ffn"], sc2, sh2, gt2, g_final, w["w_gate_up"], w["w_down"], ffn_rows)
    heads_of = lambda t: t.reshape(batch, seq, N_HEADS, HEAD_DIM)
    return y, heads_of(ka4), heads_of(va4), heads_of(kb4), heads_of(vb4)


def kernel(x_prompt, x_sample, c_prompt, c_sample, cache_a_k, cache_a_v, cache_b_k, cache_b_v,
           w_ada, b_ada, g_mix, w_in, rel_bias, w_a_out, w_b_out, w_o, g_ffn, w_gate_up, w_down,
           g_final):
    depth = w_in.shape[0]
    assert depth == 1
    batch, seq, d = x_prompt.shape
    dec_batch, dec_seq, _ = x_sample.shape
    past_len = cache_b_k.shape[2]
    l = 0

    n_c = batch + dec_batch
    c_rows = -(-n_c // 8) * 8
    c_all = jnp.concatenate([c_prompt, c_sample, jnp.zeros((c_rows - n_c, d), F32)], axis=0)
    mod_all = _ada(c_all, w_ada[l], b_ada[l])

    def mods_of(rows):
        return tuple(m[:, None, :] for m in jnp.split(rows, 6, axis=-1))

    mods_p = mods_of(mod_all[:batch])
    mods_s = mods_of(mod_all[batch:n_c])

    w = {
        "g_mix": g_mix[l], "g_ffn": g_ffn[l],
        "w_in": w_in[l].astype(BF16), "w_a_out": w_a_out[l].astype(BF16),
        "w_b_out": w_b_out[l].astype(BF16), "w_o": w_o[l].astype(BF16),
        "w_gate_up": w_gate_up[l].astype(BF16), "w_down": w_down[l].astype(BF16),
    }

    yp, ka, va, kb, vb = _layer(
        x_prompt, mods_p, w,
        functools.partial(_attn_a_prompt, rel_bias=rel_bias[l], batch=batch, seq=seq),
        functools.partial(_attn_b_prompt, batch=batch, seq=seq),
        g_final, in_rows=512, mix_rows=256, ffn_rows=1024)
    keep = min(BAND, seq)
    outs_p = (ka[:, -keep:], va[:, -keep:], kb, vb)

    la = cache_a_k.shape[2]
    cak = cache_a_k[l].reshape(dec_batch, la, -1)
    cav = cache_a_v[l].reshape(dec_batch, la, -1)
    cbk = cache_b_k[l].reshape(dec_batch, past_len, -1)
    cbv = cache_b_v[l].reshape(dec_batch, past_len, -1)
    ys, ka_s, va_s, kb_s, vb_s = _layer(
        x_sample, mods_s, w,
        functools.partial(_attn_a_sample, cache_k=cak, cache_v=cav, rel_bias=rel_bias[l],
                          batch=dec_batch, n=dec_seq, past_len=past_len),
        functools.partial(_attn_b_sample, cache_k=cbk, cache_v=cbv, batch=dec_batch, n=dec_seq),
        g_final, in_rows=dec_batch * dec_seq, mix_rows=dec_batch * dec_seq,
        ffn_rows=dec_batch * dec_seq)
    outs_s = (ka_s, va_s, kb_s, vb_s)

    return (yp, ys) + tuple(t[None] for t in outs_p) + tuple(t[None] for t in outs_s)
```

```python
import functools

import jax
import jax.numpy as jnp
from jax import lax
from jax.experimental import pallas as pl
from jax.experimental.pallas import tpu as pltpu

F32 = jnp.float32
BF16 = jnp.bfloat16

HEAD_DIM = 128
CHUNK = 64
BAND_CHUNKS = 8
BAND = BAND_CHUNKS * CHUNK
REL_CLIP = 128
EPS = 1e-6
ATTN_SCALE = HEAD_DIM ** -0.5
LOG2_E = 1.4426950408889634
MASK_VALUE = -1e30

V7X_VMEM_BYTES = 64 * 1024 * 1024
V7X_LANES = 128
V7X_BF16_SUBLANES = 16

EXP_ZERO_BELOW = -104.0

NORM_ROWS = 16
NORM_UNROLL = 8


def _norm_unroll(n_rows):
    steps = n_rows // NORM_ROWS
    return NORM_UNROLL if steps % NORM_UNROLL == 0 else 1


def _vmem_limit(estimate_bytes):
    return int(min(V7X_VMEM_BYTES - (4 << 20), max(estimate_bytes * 5 // 4, 16 << 20)))


def _dot(a, b):
    return jnp.dot(a, b, preferred_element_type=F32)


def _dot_nt(a, b):
    return lax.dot_general(a, b, (((1,), (1,)), ((), ())), preferred_element_type=F32)


def _sigmoid(x):
    return 1.0 / (1.0 + jnp.exp(-x))


def _ada_kernel(c_ref, w_ref, b_ref, o_ref):
    c = c_ref[...]
    a = (c * _sigmoid(c)).astype(BF16)
    o_ref[...] = _dot(a, w_ref[...].astype(BF16)) + b_ref[...]


def _ada(c_all, w_ada, b_ada):
    rows, d = c_all.shape
    n = w_ada.shape[1]
    tn = 1024
    est = 2 * d * tn * 4 + d * tn * 2 + 4 * rows * (d + tn) * 4
    return pl.pallas_call(
        _ada_kernel,
        grid=(n // tn,),
        in_specs=[
            pl.BlockSpec((rows, d), lambda j: (0, 0)),
            pl.BlockSpec((d, tn), lambda j: (0, j)),
            pl.BlockSpec((1, tn), lambda j: (0, j)),
        ],
        out_specs=pl.BlockSpec((rows, tn), lambda j: (0, j)),
        out_shape=jax.ShapeDtypeStruct((rows, n), F32),
        compiler_params=pltpu.CompilerParams(
            dimension_semantics=("parallel",), vmem_limit_bytes=_vmem_limit(est)),
        name="ada",
    )(c_all, w_ada, b_ada.reshape(1, n))


def _norm_mod_rows(x_ref, g_ref, sc_ref, sh_ref, h_ref, groups, ts):
    gvec = g_ref[...]
    for gi in range(groups):
        scale = gvec * (1.0 + sc_ref[gi])
        shift = sh_ref[gi]

        def body(r, carry, gi=gi, scale=scale, shift=shift):
            r0 = pl.multiple_of(r * NORM_ROWS, NORM_ROWS)
            x = x_ref[gi, pl.ds(r0, NORM_ROWS), :]
            ms = jnp.mean(x * x, axis=-1, keepdims=True)
            h = x * lax.rsqrt(ms + EPS) * scale + shift
            h_ref[pl.ds(pl.multiple_of(gi * ts + r0, NORM_ROWS), NORM_ROWS), :] = h.astype(BF16)
            return carry

        lax.fori_loop(0, ts // NORM_ROWS, body, 0, unroll=_norm_unroll(ts))


def _row_tiling(batch, seq, tile_rows):
    if seq >= tile_rows:
        assert seq % tile_rows == 0
        groups, ts, ns = 1, tile_rows, seq // tile_rows
    else:
        assert tile_rows % seq == 0 and batch % (tile_rows // seq) == 0
        groups, ts, ns = tile_rows // seq, seq, 1
    n_tiles = batch * seq // (groups * ts)
    if groups == 1:
        x_map = lambda i: (i // ns, i % ns, 0)
        b_map = lambda i: (i // ns, 0, 0)
    else:
        x_map = lambda i: (i, 0, 0)
        b_map = lambda i: (i, 0, 0)
    return groups, ts, n_tiles, x_map, b_map


N_QKV = 6
N_HEADS = 8


def _inproj_kernel(x_ref, g_ref, sc_ref, sh_ref, w_ref,
                   qa_ref, qb_ref, ka_ref, va_ref, kb_ref, vb_ref,
                   ka4_ref, va4_ref, kb4_ref, vb4_ref, gates_ref, h_ref, *, groups, ts):
    j = pl.program_id(1)
    tm = groups * ts

    @pl.when(j == 0)
    def _():
        _norm_mod_rows(x_ref, g_ref, sc_ref, sh_ref, h_ref, groups, ts)

    for idx, o_ref, scale in ((0, qa_ref, ATTN_SCALE * LOG2_E), (3, qb_ref, ATTN_SCALE)):
        @pl.when(j == idx)
        def _(o_ref=o_ref, scale=scale):
            o_ref[...] = (_dot(h_ref[...], w_ref[...]) * scale).astype(o_ref.dtype)

    for idx, o_ref, o4_ref in ((1, ka_ref, ka4_ref), (2, va_ref, va4_ref),
                               (4, kb_ref, kb4_ref), (5, vb_ref, vb4_ref)):
        @pl.when(j == idx)
        def _(o_ref=o_ref, o4_ref=o4_ref):
            p = _dot(h_ref[...], w_ref[...])
            o_ref[...] = p.astype(o_ref.dtype)
            for hd in range(N_HEADS):
                o4_ref[pl.ds(hd, tm, stride=N_HEADS), :] = p[:, hd * HEAD_DIM:(hd + 1) * HEAD_DIM]

    @pl.when(j >= N_QKV)
    def _():
        gates_ref[...] = _dot(h_ref[...], w_ref[...]).astype(gates_ref.dtype)


def _inproj(x, g, sc, sh, w_in, tile_rows, width):
    batch, seq, d = x.shape
    rows = batch * seq
    n_cols = w_in.shape[1]
    assert width == N_HEADS * HEAD_DIM and n_cols % width == 0
    nj = n_cols // width
    n_gate_tiles = nj - N_QKV
    groups, ts, n_tiles, x_map, b_map = _row_tiling(batch, seq, tile_rows)
    tm = groups * ts
    xm = lambda i, j: x_map(i)
    bm = lambda i, j: b_map(i)
    row_spec = pl.BlockSpec((tm, width), lambda i, j: (i, 0))
    flat_spec = pl.BlockSpec((tm * N_HEADS, HEAD_DIM), lambda i, j: (i, 0))
    est = (2 * tm * d * 4 + 2 * d * width * 2 + tm * d * 2
           + 2 * tm * width * (2 * 2 + 4 * 2 + 4 * 4 + 2) + 2 * tm * width * 4)
    bf16_rows = jax.ShapeDtypeStruct((rows, width), BF16)
    f32_flat = jax.ShapeDtypeStruct((rows * N_HEADS, HEAD_DIM), F32)
    return pl.pallas_call(
        functools.partial(_inproj_kernel, groups=groups, ts=ts),
        grid=(n_tiles, nj),
        in_specs=[
            pl.BlockSpec((groups, ts, d), xm),
            pl.BlockSpec((1, d), lambda i, j: (0, 0)),
            pl.BlockSpec((groups, 1, d), bm),
            pl.BlockSpec((groups, 1, d), bm),
            pl.BlockSpec((d, width), lambda i, j: (0, j)),
        ],
        out_specs=[row_spec] * 6 + [flat_spec] * 4 + [
            pl.BlockSpec((tm, width), lambda i, j: (i, jnp.maximum(j - N_QKV, 0)))],
        out_shape=[bf16_rows] * 6 + [f32_flat] * 4
        + [jax.ShapeDtypeStruct((rows, n_gate_tiles * width), BF16)],
        scratch_shapes=[pltpu.VMEM((tm, d), BF16)],
        compiler_params=pltpu.CompilerParams(
            dimension_semantics=("parallel", "arbitrary"), vmem_limit_bytes=_vmem_limit(est)),
        name="inproj",
    )(x, g.reshape(1, d), sc, sh, w_in)


A_BLOCK = 256


A_HEADS = 4
A_VARIANTS = 3
TOEPLITZ_ROW = 1024


def _toeplitz_row(rel_bias, offset, n_cols):
    m = jnp.arange(TOEPLITZ_ROW)
    m = jnp.where(m < n_cols, m, m - TOEPLITZ_ROW)
    idx = jnp.clip(offset - m, -REL_CLIP, REL_CLIP) + REL_CLIP
    return rel_bias.astype(F32)[:, None, idx]


def _toeplitz_bias(row, n_rows):
    return pltpu.roll(jnp.broadcast_to(row, (n_rows, TOEPLITZ_ROW)), 0, 1, stride=1, stride_axis=0)


def _attn_a_prompt_kernel(g_ref, q_ref, k0_ref, k1_ref, k2_ref, v0_ref, v1_ref, v2_ref, o_ref, bias_ref):
    qi = pl.program_id(2)
    n_keys = 3 * A_BLOCK

    @pl.when(qi == 0)
    def _():
        qc = lax.broadcasted_iota(jnp.int32, (A_BLOCK, n_keys), 0) // CHUNK
        col = lax.broadcasted_iota(jnp.int32, (A_BLOCK, n_keys), 1)
        kc = col // CHUNK - BAND // CHUNK
        in_band = jnp.logical_and(kc <= qc, kc >= qc - BAND_CHUNKS)
        for hh in range(A_HEADS):
            rel = _toeplitz_bias(g_ref[hh], A_BLOCK)[:, :n_keys] * LOG2_E
            full = jnp.where(in_band, rel, MASK_VALUE)
            for v in range(A_VARIANTS):
                bias_ref[v, hh] = jnp.where(col >= (A_VARIANTS - 1 - v) * A_BLOCK, full, MASK_VALUE)

    variant = jnp.minimum(qi, A_VARIANTS - 1)
    for hh in range(A_HEADS):
        cols = slice(hh * HEAD_DIM, (hh + 1) * HEAD_DIM)
        q = q_ref[:, cols]
        scores = [
            _dot_nt(q, k_ref[:, cols].astype(BF16))
            + bias_ref[variant, hh, :, jb * A_BLOCK:(jb + 1) * A_BLOCK]
            for jb, k_ref in enumerate((k0_ref, k1_ref, k2_ref))]
        m = jnp.max(functools.reduce(jnp.maximum, scores), axis=-1, keepdims=True)
        es = [jnp.exp2(s - m) for s in scores]
        l = jnp.sum(functools.reduce(jnp.add, es), axis=-1, keepdims=True)
        acc = functools.reduce(jnp.add, [
            _dot(e.astype(BF16), v_ref[:, cols].astype(BF16))
            for e, v_ref in zip(es, (v0_ref, v1_ref, v2_ref))])
        o_ref[:, cols] = (acc / l).astype(o_ref.dtype)


def _attn_a_prompt(q, k, v, rel_bias, batch, seq):
    rows, width = q.shape
    heads = width // HEAD_DIM
    nq = seq // A_BLOCK
    assert BAND == 2 * A_BLOCK and seq % A_BLOCK == 0 and heads % A_HEADS == 0
    assert 4 * A_BLOCK - 1 <= TOEPLITZ_ROW
    g = _toeplitz_row(rel_bias, BAND, 3 * A_BLOCK)
    blk = (A_BLOCK, A_HEADS * HEAD_DIM)

    def kv_spec(jb):
        return pl.BlockSpec(blk, lambda b, h, i: (b * nq + jnp.maximum(i + (jb - 2), 0), h))

    est = (2 * A_BLOCK * A_HEADS * HEAD_DIM * (2 + 6 * 4 + 2)
           + A_HEADS * (A_VARIANTS * A_BLOCK * 3 * A_BLOCK * 4 + 8 * A_BLOCK * 3 * A_BLOCK * 4))
    return pl.pallas_call(
        _attn_a_prompt_kernel,
        grid=(batch, heads // A_HEADS, nq),
        in_specs=[pl.BlockSpec((A_HEADS, 1, TOEPLITZ_ROW), lambda b, h, i: (h, 0, 0)),
                  pl.BlockSpec(blk, lambda b, h, i: (b * nq + i, h)),
                  kv_spec(0), kv_spec(1), kv_spec(2), kv_spec(0), kv_spec(1), kv_spec(2)],
        out_specs=pl.BlockSpec(blk, lambda b, h, i: (b * nq + i, h)),
        out_shape=jax.ShapeDtypeStruct((rows, width), BF16),
        scratch_shapes=[pltpu.VMEM((A_VARIANTS, A_HEADS, A_BLOCK, 3 * A_BLOCK), F32)],
        compiler_params=pltpu.CompilerParams(
            dimension_semantics=("parallel", "parallel", "arbitrary"),
            vmem_limit_bytes=_vmem_limit(est)),
        name="attn_a_prompt",
    )(g, q, k, k, k, v, v, v)


def _head_rows(ref, first_pos, n_pos, head):
    return ref[pl.ds(first_pos * N_HEADS + head, n_pos, stride=N_HEADS), :]


def _attn_a_sample_kernel(g_ref, q_ref, kc_ref, vc_ref, kn_ref, vn_ref, o_ref, *, lc):
    n = q_ref.shape[0]
    kc_ref = kc_ref.at[0]
    vc_ref = vc_ref.at[0]
    for hh in range(N_HEADS):
        cols = slice(hh * HEAD_DIM, (hh + 1) * HEAD_DIM)
        q = q_ref[:, cols]
        bias = _toeplitz_bias(g_ref[hh], n) * LOG2_E
        sc = _dot_nt(q, _head_rows(kc_ref, 0, lc, hh).astype(BF16)) + bias[:, :lc]
        sn = _dot_nt(q, kn_ref[:, cols]) + bias[:, lc:lc + n]
        m = jnp.maximum(jnp.max(sc, axis=-1, keepdims=True), jnp.max(sn, axis=-1, keepdims=True))
        ec = jnp.exp2(sc - m)
        en = jnp.exp2(sn - m)
        l = jnp.sum(ec, axis=-1, keepdims=True) + jnp.sum(en, axis=-1, keepdims=True)
        acc = (_dot(ec.astype(BF16), _head_rows(vc_ref, 0, lc, hh).astype(BF16))
               + _dot(en.astype(BF16), vn_ref[:, cols]))
        o_ref[:, cols] = (acc / l).astype(o_ref.dtype)


def _attn_a_sample(q, k, v, cache_k, cache_v, rel_bias, batch, n, past_len):
    rows, width = q.shape
    lc = cache_k.shape[1] // N_HEADS
    assert past_len >= lc and lc + 2 * n - 1 <= TOEPLITZ_ROW and lc % V7X_LANES == 0
    assert width == N_HEADS * HEAD_DIM
    g = _toeplitz_row(rel_bias, lc, lc + n)
    blk = pl.BlockSpec((n, width), lambda b: (b, 0))
    cblk = pl.BlockSpec((1, lc * N_HEADS, HEAD_DIM), lambda b: (b, 0, 0))
    est = 2 * 2 * lc * width * 4 + 16 * n * lc * 4 * N_HEADS
    return pl.pallas_call(
        functools.partial(_attn_a_sample_kernel, lc=lc),
        grid=(batch,),
        in_specs=[pl.BlockSpec((N_HEADS, 1, TOEPLITZ_ROW), lambda b: (0, 0, 0)),
                  blk, cblk, cblk, blk, blk],
        out_specs=blk,
        out_shape=jax.ShapeDtypeStruct((rows, width), BF16),
        compiler_params=pltpu.CompilerParams(
            dimension_semantics=("parallel",), vmem_limit_bytes=_vmem_limit(est)),
        name="attn_a_sample",
    )(g, q, cache_k, cache_v, k, v)


B_BLOCK = 256
B_HEADS = 2


def _later_matrix(n):
    j = jnp.arange(n)[:, None]
    s = jnp.arange(n)[None, :]
    return (j > s).astype(BF16)


def _strict_causal_mask(n):
    t = jnp.arange(n)[:, None]
    s = jnp.arange(n)[None, :]
    return jnp.where(s < t, 0.0, MASK_VALUE).astype(F32)


def _stick_terms(q, k, later_mat, z_mask=None, valid=None):
    z = _dot_nt(q, k)
    if z_mask is not None:
        z = z + z_mask
    if valid is not None:
        z = jnp.where(valid, z, MASK_VALUE)
    log_1m = -(jnp.maximum(z, 0.0) + jnp.log(1.0 + jnp.exp(-jnp.abs(z))))
    hi = log_1m.astype(BF16)
    lo = (log_1m - hi.astype(F32)).astype(BF16)
    later = _dot(hi, later_mat) + _dot(lo, later_mat)
    return z + log_1m + later, jnp.sum(log_1m, axis=-1, keepdims=True)


def _stick_sweep_past(qs, load_block, n_blocks, cs, accs, later_mat):
    heads = len(qs)

    def cond(carry):
        j, cs, _ = carry
        return jnp.logical_and(j >= 0, jnp.max(functools.reduce(jnp.maximum, cs)) > EXP_ZERO_BELOW)

    def body(carry):
        j, cs, accs = carry
        k_of, v_of = load_block(j)
        new_cs, new_accs = [], []
        for hh in range(heads):
            logit, row_sum = _stick_terms(qs[hh], k_of(hh).astype(BF16), later_mat)
            weights = jnp.exp(logit + cs[hh]).astype(BF16)
            new_accs.append(accs[hh] + _dot(weights, v_of(hh).astype(BF16)))
            new_cs.append(cs[hh] + row_sum)
        return j - 1, tuple(new_cs), tuple(new_accs)

    _, _, accs = lax.while_loop(cond, body, (n_blocks - 1, tuple(cs), tuple(accs)))
    return accs


def _stick_heads(q_ref, own, prev, later_own, mask_own, later_mat, prev_valid, load_block, n_blocks,
                 o_ref):
    heads = q_ref.shape[1] // HEAD_DIM
    qs, cs, accs = [], [], []
    for hh in range(heads):
        q = q_ref[:, hh * HEAD_DIM:(hh + 1) * HEAD_DIM]
        logit0, sum0 = _stick_terms(q, own[0](hh).astype(BF16), later_own, z_mask=mask_own)
        logit1, sum1 = _stick_terms(q, prev[0](hh).astype(BF16), later_mat, valid=prev_valid)
        accs.append(_dot(jnp.exp(logit0).astype(BF16), own[1](hh).astype(BF16))
                    + _dot(jnp.exp(logit1 + sum0).astype(BF16), prev[1](hh).astype(BF16)))
        cs.append(sum0 + sum1)
        qs.append(q)
    accs = _stick_sweep_past(qs, load_block, n_blocks, cs, accs, later_mat)
    for hh in range(heads):
        o_ref[:, hh * HEAD_DIM:(hh + 1) * HEAD_DIM] = accs[hh].astype(o_ref.dtype)


def _lane_heads(k_ref, v_ref, rows):
    cols = lambda hh: slice(hh * HEAD_DIM, (hh + 1) * HEAD_DIM)
    return (lambda hh: k_ref[rows, cols(hh)]), (lambda hh: v_ref[rows, cols(hh)])


def _row_heads(k_ref, v_ref, n_pos):
    return (lambda hh: _head_rows(k_ref, 0, n_pos, hh)), (lambda hh: _head_rows(v_ref, 0, n_pos, hh))


def _attn_b_prompt_kernel(q_ref, k_ref, v_ref, u_ref, mask_ref, o_ref):
    qi = pl.program_id(2)
    k_ref = k_ref.at[0]
    v_ref = v_ref.at[0]
    later_mat = u_ref[...]
    block = lambda j: _lane_heads(k_ref, v_ref, pl.ds(pl.multiple_of(j * B_BLOCK, B_BLOCK), B_BLOCK))
    _stick_heads(q_ref, block(qi), block(jnp.maximum(qi - 1, 0)), later_mat, mask_ref[...], later_mat,
                 qi > 0, block, qi - 1, o_ref)


def _attn_b_prompt(q, k, v, batch, seq):
    rows, width = q.shape
    heads = width // HEAD_DIM
    assert heads % B_HEADS == 0
    nq = seq // B_BLOCK
    k3 = k.reshape(batch, seq, width)
    v3 = v.reshape(batch, seq, width)
    lanes = B_HEADS * HEAD_DIM
    blk = (B_BLOCK, lanes)
    kv_spec = pl.BlockSpec((1, seq, lanes), lambda b, h, i: (b, 0, h))
    const = pl.BlockSpec((B_BLOCK, B_BLOCK), lambda b, h, i: (0, 0))
    est = 2 * 2 * seq * lanes * k.dtype.itemsize + B_HEADS * 40 * B_BLOCK * B_BLOCK * 4
    return pl.pallas_call(
        _attn_b_prompt_kernel,
        grid=(batch, heads // B_HEADS, nq),
        in_specs=[pl.BlockSpec(blk, lambda b, h, i: (b * nq + i, h)), kv_spec, kv_spec, const, const],
        out_specs=pl.BlockSpec(blk, lambda b, h, i: (b * nq + i, h)),
        out_shape=jax.ShapeDtypeStruct((rows, width), BF16),
        compiler_params=pltpu.CompilerParams(
            dimension_semantics=("parallel", "parallel", "parallel"),
            vmem_limit_bytes=_vmem_limit(est)),
        name="attn_b_prompt",
    )(q, k3, v3, _later_matrix(B_BLOCK), _strict_causal_mask(B_BLOCK))


def _attn_b_sample_kernel(q_ref, kn_ref, vn_ref, k1_ref, v1_ref, kc_hbm, vc_hbm, un_ref, mask_ref, u_ref,
                          o_ref, kbuf, vbuf, sems, *, n_blocks):
    b = pl.program_id(0)
    block_rows = B_BLOCK * N_HEADS

    def fetch_block(j):
        rows = pl.ds(pl.multiple_of(j * block_rows, block_rows), block_rows)
        copies = (pltpu.make_async_copy(kc_hbm.at[b, rows, :], kbuf, sems.at[0]),
                  pltpu.make_async_copy(vc_hbm.at[b, rows, :], vbuf, sems.at[1]))
        for cp in copies:
            cp.start()
        for cp in copies:
            cp.wait()
        return _row_heads(kbuf, vbuf, B_BLOCK)

    _stick_heads(q_ref, _lane_heads(kn_ref, vn_ref, slice(None)),
                 _row_heads(k1_ref.at[0], v1_ref.at[0], B_BLOCK),
                 un_ref[...], mask_ref[...], u_ref[...], None, fetch_block, n_blocks - 1, o_ref)


def _attn_b_sample(q, k, v, cache_k, cache_v, batch, n):
    rows, width = q.shape
    lc = cache_k.shape[1] // N_HEADS
    assert lc % B_BLOCK == 0 and lc >= B_BLOCK and width == N_HEADS * HEAD_DIM
    n_blocks = lc // B_BLOCK
    block_rows = B_BLOCK * N_HEADS
    blk = pl.BlockSpec((n, width), lambda b: (b, 0))
    newest = pl.BlockSpec((1, block_rows, HEAD_DIM), lambda b: (b, n_blocks - 1, 0))
    anywhere = pl.BlockSpec(memory_space=pl.ANY)
    const = lambda shape: pl.BlockSpec(shape, lambda b: (0, 0))
    est = 6 * block_rows * HEAD_DIM * 4 + N_HEADS * 40 * n * B_BLOCK * 4 + 8 * B_BLOCK * B_BLOCK
    return pl.pallas_call(
        functools.partial(_attn_b_sample_kernel, n_blocks=n_blocks),
        grid=(batch,),
        in_specs=[blk, blk, blk, newest, newest, anywhere, anywhere,
                  const((n, n)), const((n, n)), const((B_BLOCK, B_BLOCK))],
        out_specs=blk,
        out_shape=jax.ShapeDtypeStruct((rows, width), BF16),
        scratch_shapes=[pltpu.VMEM((block_rows, HEAD_DIM), F32), pltpu.VMEM((block_rows, HEAD_DIM), F32),
                        pltpu.SemaphoreType.DMA((2,))],
        compiler_params=pltpu.CompilerParams(
            dimension_semantics=("arbitrary",), vmem_limit_bytes=_vmem_limit(est)),
        name="attn_b_sample",
    )(q, k, v, cache_k, cache_v, cache_k, cache_v,
      _later_matrix(n), _strict_causal_mask(n), _later_matrix(B_BLOCK))


MIX_COLS = 512


def _mix_out_kernel(x_ref, oa_ref, ob_ref, gates_ref, gt_ref, wa_ref, wb_ref, wo_ref, o_ref, y_ref,
                    *, groups, ts):
    d = o_ref.shape[-1]
    oa = oa_ref[...]
    ob = ob_ref[...]
    for c0 in range(0, d, MIX_COLS):
        cols = slice(c0, c0 + MIX_COLS)
        ya = _dot(oa, wa_ref[:, cols])
        yb = _dot(ob, wb_ref[:, cols])
        ga = gates_ref[:, cols].astype(F32)
        gb = gates_ref[:, d + c0:d + c0 + MIX_COLS].astype(F32)
        y_ref[:, cols] = (_sigmoid(ga) * ya + _sigmoid(gb) * yb).astype(BF16)
    y = y_ref[...]
    for c0 in range(0, d, MIX_COLS):
        cols = slice(c0, c0 + MIX_COLS)
        merged = _dot(y, wo_ref[:, cols])
        for gi in range(groups):
            rows = slice(gi * ts, (gi + 1) * ts)
            o_ref[gi, :, cols] = x_ref[gi, :, cols] + gt_ref[gi, :, cols] * merged[rows]


def _mix_out(x, oa, ob, gates, gt, wa, wb, wo, tile_rows):
    batch, seq, d = x.shape
    width = oa.shape[1]
    groups, ts, n_tiles, x_map, b_map = _row_tiling(batch, seq, tile_rows)
    tm = groups * ts
    const = lambda i: (0, 0)
    single = pl.Buffered(1)
    est = (2 * 2 * tm * d * 4 + 2 * 2 * tm * width * 2 + 2 * tm * 2 * d * 2 + tm * d * 2
           + (2 * width * d + d * d) * 2 + 6 * tm * MIX_COLS * 4)
    return pl.pallas_call(
        functools.partial(_mix_out_kernel, groups=groups, ts=ts),
        grid=(n_tiles,),
        in_specs=[
            pl.BlockSpec((groups, ts, d), x_map),
            pl.BlockSpec((tm, width), lambda i: (i, 0)),
            pl.BlockSpec((tm, width), lambda i: (i, 0)),
            pl.BlockSpec((tm, 2 * d), lambda i: (i, 0)),
            pl.BlockSpec((groups, 1, d), b_map),
            pl.BlockSpec((width, d), const, pipeline_mode=single),
            pl.BlockSpec((width, d), const, pipeline_mode=single),
            pl.BlockSpec((d, d), const, pipeline_mode=single),
        ],
        out_specs=pl.BlockSpec((groups, ts, d), x_map),
        out_shape=jax.ShapeDtypeStruct((batch, seq, d), F32),
        scratch_shapes=[pltpu.VMEM((tm, d), BF16)],
        compiler_params=pltpu.CompilerParams(
            dimension_semantics=("parallel",), vmem_limit_bytes=_vmem_limit(est)),
        name="mix_out",
    )(x, oa, ob, gates, gt, wa, wb, wo)


FF_TILE = 512


def _ffn_kernel(x_ref, g_ref, sc_ref, sh_ref, gt_ref, gf_ref, wg_ref, wu_ref, wd_ref, o_ref,
                h_ref, *, groups, ts):
    f = pl.program_id(1)

    @pl.when(f == 0)
    def _():
        _norm_mod_rows(x_ref, g_ref, sc_ref, sh_ref, h_ref, groups, ts)
        o_ref[...] = jnp.zeros_like(o_ref)

    h = h_ref[...]
    gate = _dot(h, wg_ref[...])
    up = _dot(h, wu_ref[...])
    hidden = (gate * _sigmoid(gate) * up).astype(BF16)
    if groups == 1:
        o_ref[0] += _dot(hidden, wd_ref[...])
    else:
        part = _dot(hidden, wd_ref[...])
        for gi in range(groups):
            o_ref[gi] += part[gi * ts:(gi + 1) * ts]

    @pl.when(f == pl.num_programs(1) - 1)
    def _():
        gf = gf_ref[...]
        for gi in range(groups):
            gt = gt_ref[gi]
            for r0 in range(0, ts, NORM_ROWS):
                rows = slice(r0, r0 + NORM_ROWS)
                x2 = x_ref[gi, rows, :] + gt * o_ref[gi, rows, :]
                ms = jnp.mean(x2 * x2, axis=-1, keepdims=True)
                o_ref[gi, rows, :] = x2 * lax.rsqrt(ms + EPS) * gf


def _ffn(x, g, sc, sh, gt, g_final, w_gate_up, w_down, tile_rows):
    batch, seq, d = x.shape
    d_ff = w_down.shape[0]
    nf = d_ff // FF_TILE
    groups, ts, n_tiles, x_map, b_map = _row_tiling(batch, seq, tile_rows)
    tm = groups * ts
    xm = lambda i, f: x_map(i)
    bm = lambda i, f: b_map(i)
    vec = pl.BlockSpec((1, d), lambda i, f: (0, 0))
    mod = pl.BlockSpec((groups, 1, d), bm)
    est = 3 * tm * d * 4 + tm * d * 2 + 2 * 3 * d * FF_TILE * 2 + 5 * tm * FF_TILE * 4
    return pl.pallas_call(
        functools.partial(_ffn_kernel, groups=groups, ts=ts),
        grid=(n_tiles, nf),
        in_specs=[
            pl.BlockSpec((groups, ts, d), xm, pipeline_mode=pl.Buffered(1)), vec, mod, mod, mod, vec,
            pl.BlockSpec((d, FF_TILE), lambda i, f: (0, f)),
            pl.BlockSpec((d, FF_TILE), lambda i, f: (0, nf + f)),
            pl.BlockSpec((FF_TILE, d), lambda i, f: (f, 0)),
        ],
        out_specs=pl.BlockSpec((groups, ts, d), xm),
        out_shape=jax.ShapeDtypeStruct((batch, seq, d), F32),
        scratch_shapes=[pltpu.VMEM((tm, d), BF16)],
        compiler_params=pltpu.CompilerParams(
            dimension_semantics=("parallel", "arbitrary"), vmem_limit_bytes=_vmem_limit(est)),
        name="ffn",
    )(x, g.reshape(1, d), sc, sh, gt, g_final.reshape(1, d), w_gate_up, w_gate_up, w_down)


def _layer(x, mods, w, attn_a, attn_b, g_final, in_rows, mix_rows, ffn_rows):
    batch, seq, d = x.shape
    sh1, sc1, gt1, sh2, sc2, gt2 = mods
    width = w["w_a_out"].shape[0]
    qa, qb, ka, va, kb, vb, ka4, va4, kb4, vb4, gates = _inproj(
        x, w["g_mix"], sc1, sh1, w["w_in"], in_rows, width)
    oa = attn_a(qa, ka, va)
    ob = attn_b(qb, kb, vb)
    x1 = _mix_out(x, oa, ob, gates, gt1, w["w_a_out"], w["w_b_out"], w["w_o"], mix_rows)
    y = _ffn(x1, w["g_ffn"], sc2, sh2, gt2, g_final, w["w_gate_up"], w["w_down"], ffn_rows)
    heads_of = lambda t: t.reshape(batch, seq, N_HEADS, HEAD_DIM)
    return y, heads_of(ka4), heads_of(va4), heads_of(kb4), heads_of(vb4)


def kernel(x_prompt, x_sample, c_prompt, c_sample, cache_a_k, cache_a_v, cache_b_k, cache_b_v,
           w_ada, b_ada, g_mix, w_in, rel_bias, w_a_out, w_b_out, w_o, g_ffn, w_gate_up, w_down,
           g_final):
    depth = w_in.shape[0]
    assert depth == 1
    batch, seq, d = x_prompt.shape
    dec_batch, dec_seq, _ = x_sample.shape
    past_len = cache_b_k.shape[2]
    l = 0

    n_c = batch + dec_batch
    c_rows = -(-n_c // 8) * 8
    c_all = jnp.concatenate([c_prompt, c_sample, jnp.zeros((c_rows - n_c, d), F32)], axis=0)
    mod_all = _ada(c_all, w_ada[l], b_ada[l])

    def mods_of(rows):
        return tuple(m[:, None, :] for m in jnp.split(rows, 6, axis=-1))

    mods_p = mods_of(mod_all[:batch])
    mods_s = mods_of(mod_all[batch:n_c])

    w = {
        "g_mix": g_mix[l], "g_ffn": g_ffn[l],
        "w_in": w_in[l].astype(BF16), "w_a_out": w_a_out[l].astype(BF16),
        "w_b_out": w_b_out[l].astype(BF16), "w_o": w_o[l].astype(BF16),
        "w_gate_up": w_gate_up[l].astype(BF16), "w_down": w_down[l].astype(BF16),
    }

    yp, ka, va, kb, vb = _layer(
        x_prompt, mods_p, w,
        functools.partial(_attn_a_prompt, rel_bias=rel_bias[l], batch=batch, seq=seq),
        functools.partial(_attn_b_prompt, batch=batch, seq=seq),
        g_final, in_rows=512, mix_rows=256, ffn_rows=1024)
    keep = min(BAND, seq)
    outs_p = (ka[:, -keep:], va[:, -keep:], kb, vb)

    flat_rows = lambda c: c[l].reshape(dec_batch, c.shape[2] * c.shape[3], c.shape[4])
    cak, cav, cbk, cbv = (flat_rows(c) for c in (cache_a_k, cache_a_v, cache_b_k, cache_b_v))
    ys, ka_s, va_s, kb_s, vb_s = _layer(
        x_sample, mods_s, w,
        functools.partial(_attn_a_sample, cache_k=cak, cache_v=cav, rel_bias=rel_bias[l],
                          batch=dec_batch, n=dec_seq, past_len=past_len),
        functools.partial(_attn_b_sample, cache_k=cbk, cache_v=cbv, batch=dec_batch, n=dec_seq),
        g_final, in_rows=dec_batch * dec_seq, mix_rows=dec_batch * dec_seq,
        ffn_rows=dec_batch * dec_seq)
    outs_s = (ka_s, va_s, kb_s, vb_s)

    return (yp, ys) + tuple(t[None] for t in outs_p) + tuple(t[None] for t in outs_s)
```

```python
import functools

import jax
import jax.numpy as jnp
from jax import lax
from jax.experimental import pallas as pl
from jax.experimental.pallas import tpu as pltpu

F32 = jnp.float32
BF16 = jnp.bfloat16

HEAD_DIM = 128
CHUNK = 64
BAND_CHUNKS = 8
BAND = BAND_CHUNKS * CHUNK
REL_CLIP = 128
EPS = 1e-6
ATTN_SCALE = HEAD_DIM ** -0.5
LOG2_E = 1.4426950408889634
MASK_VALUE = -1e30

V7X_VMEM_BYTES = 64 * 1024 * 1024
V7X_LANES = 128
V7X_BF16_SUBLANES = 16

EXP_ZERO_BELOW = -104.0

NORM_ROWS = 16
NORM_UNROLL = 8


def _norm_unroll(n_rows):
    steps = n_rows // NORM_ROWS
    return NORM_UNROLL if steps % NORM_UNROLL == 0 else 1


def _vmem_limit(estimate_bytes):
    return int(min(V7X_VMEM_BYTES - (4 << 20), max(estimate_bytes * 5 // 4, 16 << 20)))


def _dot(a, b):
    return jnp.dot(a, b, preferred_element_type=F32)


def _dot_nt(a, b):
    return lax.dot_general(a, b, (((1,), (1,)), ((), ())), preferred_element_type=F32)


def _sigmoid(x):
    return 1.0 / (1.0 + jnp.exp(-x))


def _ada_kernel(c_ref, w_ref, b_ref, o_ref):
    c = c_ref[...]
    a = (c * _sigmoid(c)).astype(BF16)
    o_ref[...] = _dot(a, w_ref[...].astype(BF16)) + b_ref[...]


def _ada(c_all, w_ada, b_ada):
    rows, d = c_all.shape
    n = w_ada.shape[1]
    tn = 1024
    est = 2 * d * tn * 4 + d * tn * 2 + 4 * rows * (d + tn) * 4
    return pl.pallas_call(
        _ada_kernel,
        grid=(n // tn,),
        in_specs=[
            pl.BlockSpec((rows, d), lambda j: (0, 0)),
            pl.BlockSpec((d, tn), lambda j: (0, j)),
            pl.BlockSpec((1, tn), lambda j: (0, j)),
        ],
        out_specs=pl.BlockSpec((rows, tn), lambda j: (0, j)),
        out_shape=jax.ShapeDtypeStruct((rows, n), F32),
        compiler_params=pltpu.CompilerParams(
            dimension_semantics=("parallel",), vmem_limit_bytes=_vmem_limit(est)),
        name="ada",
    )(c_all, w_ada, b_ada.reshape(1, n))


def _norm_mod_rows(x_ref, g_ref, sc_ref, sh_ref, h_ref, groups, ts, copy_ref=None):
    gvec = g_ref[...]
    for gi in range(groups):
        scale = gvec * (1.0 + sc_ref[gi])
        shift = sh_ref[gi]

        def body(r, carry, gi=gi, scale=scale, shift=shift):
            r0 = pl.multiple_of(r * NORM_ROWS, NORM_ROWS)
            x = x_ref[gi, pl.ds(r0, NORM_ROWS), :]
            if copy_ref is not None:
                copy_ref[gi, pl.ds(r0, NORM_ROWS), :] = x
            ms = jnp.mean(x * x, axis=-1, keepdims=True)
            h = x * lax.rsqrt(ms + EPS) * scale + shift
            h_ref[pl.ds(pl.multiple_of(gi * ts + r0, NORM_ROWS), NORM_ROWS), :] = h.astype(BF16)
            return carry

        lax.fori_loop(0, ts // NORM_ROWS, body, 0, unroll=_norm_unroll(ts))


def _row_tiling(batch, seq, tile_rows):
    if seq >= tile_rows:
        assert seq % tile_rows == 0
        groups, ts, ns = 1, tile_rows, seq // tile_rows
    else:
        assert tile_rows % seq == 0 and batch % (tile_rows // seq) == 0
        groups, ts, ns = tile_rows // seq, seq, 1
    n_tiles = batch * seq // (groups * ts)
    if groups == 1:
        x_map = lambda i: (i // ns, i % ns, 0)
        b_map = lambda i: (i // ns, 0, 0)
    else:
        x_map = lambda i: (i, 0, 0)
        b_map = lambda i: (i, 0, 0)
    return groups, ts, n_tiles, x_map, b_map


N_QKV = 6
N_HEADS = 8


def _inproj_kernel(x_ref, g_ref, sc_ref, sh_ref, w_ref,
                   qa_ref, qb_ref, ka_ref, va_ref, kb_ref, vb_ref,
                   ka4_ref, va4_ref, kb4_ref, vb4_ref, gates_ref, h_ref, *, groups, ts):
    j = pl.program_id(1)
    tm = groups * ts

    @pl.when(j == 0)
    def _():
        _norm_mod_rows(x_ref, g_ref, sc_ref, sh_ref, h_ref, groups, ts)

    for idx, o_ref, scale in ((0, qa_ref, ATTN_SCALE * LOG2_E), (3, qb_ref, ATTN_SCALE)):
        @pl.when(j == idx)
        def _(o_ref=o_ref, scale=scale):
            o_ref[...] = (_dot(h_ref[...], w_ref[...]) * scale).astype(o_ref.dtype)

    for idx, o_ref, o4_ref in ((1, ka_ref, ka4_ref), (2, va_ref, va4_ref),
                               (4, kb_ref, kb4_ref), (5, vb_ref, vb4_ref)):
        @pl.when(j == idx)
        def _(o_ref=o_ref, o4_ref=o4_ref):
            p = _dot(h_ref[...], w_ref[...])
            o_ref[...] = p.astype(o_ref.dtype)
            for hd in range(N_HEADS):
                o4_ref[pl.ds(hd, tm, stride=N_HEADS), :] = p[:, hd * HEAD_DIM:(hd + 1) * HEAD_DIM]

    @pl.when(j >= N_QKV)
    def _():
        gates_ref[...] = _dot(h_ref[...], w_ref[...]).astype(gates_ref.dtype)


def _inproj(x, g, sc, sh, w_in, tile_rows, width):
    batch, seq, d = x.shape
    rows = batch * seq
    n_cols = w_in.shape[1]
    assert width == N_HEADS * HEAD_DIM and n_cols % width == 0
    nj = n_cols // width
    n_gate_tiles = nj - N_QKV
    groups, ts, n_tiles, x_map, b_map = _row_tiling(batch, seq, tile_rows)
    tm = groups * ts
    xm = lambda i, j: x_map(i)
    bm = lambda i, j: b_map(i)
    row_spec = pl.BlockSpec((tm, width), lambda i, j: (i, 0))
    flat_spec = pl.BlockSpec((tm * N_HEADS, HEAD_DIM), lambda i, j: (i, 0))
    est = (2 * tm * d * 4 + 2 * d * width * 2 + tm * d * 2
           + 2 * tm * width * (2 * 2 + 4 * 2 + 4 * 4 + 2) + 2 * tm * width * 4)
    bf16_rows = jax.ShapeDtypeStruct((rows, width), BF16)
    f32_flat = jax.ShapeDtypeStruct((rows * N_HEADS, HEAD_DIM), F32)
    return pl.pallas_call(
        functools.partial(_inproj_kernel, groups=groups, ts=ts),
        grid=(n_tiles, nj),
        in_specs=[
            pl.BlockSpec((groups, ts, d), xm),
            pl.BlockSpec((1, d), lambda i, j: (0, 0)),
            pl.BlockSpec((groups, 1, d), bm),
            pl.BlockSpec((groups, 1, d), bm),
            pl.BlockSpec((d, width), lambda i, j: (0, j)),
        ],
        out_specs=[row_spec] * 6 + [flat_spec] * 4 + [
            pl.BlockSpec((tm, width), lambda i, j: (i, jnp.maximum(j - N_QKV, 0)))],
        out_shape=[bf16_rows] * 6 + [f32_flat] * 4
        + [jax.ShapeDtypeStruct((rows, n_gate_tiles * width), BF16)],
        scratch_shapes=[pltpu.VMEM((tm, d), BF16)],
        compiler_params=pltpu.CompilerParams(
            dimension_semantics=("parallel", "arbitrary"), vmem_limit_bytes=_vmem_limit(est)),
        name="inproj",
    )(x, g.reshape(1, d), sc, sh, w_in)


A_BLOCK = 256


A_HEADS = 8
A_VARIANTS = 3
TOEPLITZ_ROW = 1024


def _toeplitz_row(rel_bias, offset, n_cols):
    m = jnp.arange(TOEPLITZ_ROW)
    m = jnp.where(m < n_cols, m, m - TOEPLITZ_ROW)
    idx = jnp.clip(offset - m, -REL_CLIP, REL_CLIP) + REL_CLIP
    return rel_bias.astype(F32)[:, None, idx]


def _toeplitz_bias(row, n_rows):
    return pltpu.roll(jnp.broadcast_to(row, (n_rows, TOEPLITZ_ROW)), 0, 1, stride=1, stride_axis=0)


def _attn_a_prompt_kernel(g_ref, q_ref, k0_ref, k1_ref, k2_ref, v0_ref, v1_ref, v2_ref, o_ref, bias_ref):
    qi = pl.program_id(2)
    n_keys = 3 * A_BLOCK

    @pl.when(qi == 0)
    def _():
        qc = lax.broadcasted_iota(jnp.int32, (A_BLOCK, n_keys), 0) // CHUNK
        col = lax.broadcasted_iota(jnp.int32, (A_BLOCK, n_keys), 1)
        kc = col // CHUNK - BAND // CHUNK
        in_band = jnp.logical_and(kc <= qc, kc >= qc - BAND_CHUNKS)
        for hh in range(A_HEADS):
            rel = _toeplitz_bias(g_ref[hh], A_BLOCK)[:, :n_keys] * LOG2_E
            full = jnp.where(in_band, rel, MASK_VALUE)
            for v in range(A_VARIANTS):
                bias_ref[v, hh] = jnp.where(col >= (A_VARIANTS - 1 - v) * A_BLOCK, full, MASK_VALUE)

    variant = jnp.minimum(qi, A_VARIANTS - 1)
    for hh in range(A_HEADS):
        cols = slice(hh * HEAD_DIM, (hh + 1) * HEAD_DIM)
        q = q_ref[:, cols]
        scores = [
            _dot_nt(q, k_ref[:, cols].astype(BF16))
            + bias_ref[variant, hh, :, jb * A_BLOCK:(jb + 1) * A_BLOCK]
            for jb, k_ref in enumerate((k0_ref, k1_ref, k2_ref))]
        m = jnp.max(functools.reduce(jnp.maximum, scores), axis=-1, keepdims=True)
        es = [jnp.exp2(s - m) for s in scores]
        l = jnp.sum(functools.reduce(jnp.add, es), axis=-1, keepdims=True)
        acc = functools.reduce(jnp.add, [
            _dot(e.astype(BF16), v_ref[:, cols].astype(BF16))
            for e, v_ref in zip(es, (v0_ref, v1_ref, v2_ref))])
        o_ref[:, cols] = (acc / l).astype(o_ref.dtype)


def _attn_a_prompt(q, k, v, rel_bias, batch, seq):
    rows, width = q.shape
    heads = width // HEAD_DIM
    nq = seq // A_BLOCK
    assert BAND == 2 * A_BLOCK and seq % A_BLOCK == 0 and heads % A_HEADS == 0
    assert 4 * A_BLOCK - 1 <= TOEPLITZ_ROW
    g = _toeplitz_row(rel_bias, BAND, 3 * A_BLOCK)
    blk = (A_BLOCK, A_HEADS * HEAD_DIM)

    def kv_spec(jb):
        return pl.BlockSpec(blk, lambda b, h, i: (b * nq + jnp.maximum(i + (jb - 2), 0), h))

    est = (2 * A_BLOCK * A_HEADS * HEAD_DIM * (2 + 6 * 4 + 2)
           + A_HEADS * (A_VARIANTS * A_BLOCK * 3 * A_BLOCK * 4 + 8 * A_BLOCK * 3 * A_BLOCK * 4))
    return pl.pallas_call(
        _attn_a_prompt_kernel,
        grid=(batch, heads // A_HEADS, nq),
        in_specs=[pl.BlockSpec((A_HEADS, 1, TOEPLITZ_ROW), lambda b, h, i: (h, 0, 0)),
                  pl.BlockSpec(blk, lambda b, h, i: (b * nq + i, h)),
                  kv_spec(0), kv_spec(1), kv_spec(2), kv_spec(0), kv_spec(1), kv_spec(2)],
        out_specs=pl.BlockSpec(blk, lambda b, h, i: (b * nq + i, h)),
        out_shape=jax.ShapeDtypeStruct((rows, width), BF16),
        scratch_shapes=[pltpu.VMEM((A_VARIANTS, A_HEADS, A_BLOCK, 3 * A_BLOCK), F32)],
        compiler_params=pltpu.CompilerParams(
            dimension_semantics=("parallel", "parallel", "arbitrary"),
            vmem_limit_bytes=_vmem_limit(est)),
        name="attn_a_prompt",
    )(g, q, k, k, k, v, v, v)


def _head_rows(ref, first_pos, n_pos, head):
    return ref[pl.ds(first_pos * N_HEADS + head, n_pos, stride=N_HEADS), :]


def _attn_a_sample_kernel(g_ref, q_ref, kc_ref, vc_ref, kn_ref, vn_ref, o_ref, *, lc):
    n = q_ref.shape[0]
    kc_ref = kc_ref.at[0]
    vc_ref = vc_ref.at[0]
    for hh in range(N_HEADS):
        cols = slice(hh * HEAD_DIM, (hh + 1) * HEAD_DIM)
        q = q_ref[:, cols]
        bias = _toeplitz_bias(g_ref[hh], n) * LOG2_E
        sc = _dot_nt(q, _head_rows(kc_ref, 0, lc, hh).astype(BF16)) + bias[:, :lc]
        sn = _dot_nt(q, kn_ref[:, cols]) + bias[:, lc:lc + n]
        m = jnp.maximum(jnp.max(sc, axis=-1, keepdims=True), jnp.max(sn, axis=-1, keepdims=True))
        ec = jnp.exp2(sc - m)
        en = jnp.exp2(sn - m)
        l = jnp.sum(ec, axis=-1, keepdims=True) + jnp.sum(en, axis=-1, keepdims=True)
        acc = (_dot(ec.astype(BF16), _head_rows(vc_ref, 0, lc, hh).astype(BF16))
               + _dot(en.astype(BF16), vn_ref[:, cols]))
        o_ref[:, cols] = (acc / l).astype(o_ref.dtype)


def _attn_a_sample(q, k, v, cache_k, cache_v, rel_bias, batch, n, past_len):
    rows, width = q.shape
    lc = cache_k.shape[1] // N_HEADS
    assert past_len >= lc and lc + 2 * n - 1 <= TOEPLITZ_ROW and lc % V7X_LANES == 0
    assert width == N_HEADS * HEAD_DIM
    g = _toeplitz_row(rel_bias, lc, lc + n)
    blk = pl.BlockSpec((n, width), lambda b: (b, 0))
    cblk = pl.BlockSpec((1, lc * N_HEADS, HEAD_DIM), lambda b: (b, 0, 0))
    est = 2 * 2 * lc * width * 4 + 16 * n * lc * 4 * N_HEADS
    return pl.pallas_call(
        functools.partial(_attn_a_sample_kernel, lc=lc),
        grid=(batch,),
        in_specs=[pl.BlockSpec((N_HEADS, 1, TOEPLITZ_ROW), lambda b: (0, 0, 0)),
                  blk, cblk, cblk, blk, blk],
        out_specs=blk,
        out_shape=jax.ShapeDtypeStruct((rows, width), BF16),
        compiler_params=pltpu.CompilerParams(
            dimension_semantics=("parallel",), vmem_limit_bytes=_vmem_limit(est)),
        name="attn_a_sample",
    )(g, q, cache_k, cache_v, k, v)


B_BLOCK = 256
B_HEADS = 4


def _later_matrix(n):
    j = jnp.arange(n)[:, None]
    s = jnp.arange(n)[None, :]
    return (j > s).astype(BF16)


def _strict_causal_mask(n):
    t = jnp.arange(n)[:, None]
    s = jnp.arange(n)[None, :]
    return jnp.where(s < t, 0.0, MASK_VALUE).astype(F32)


def _stick_terms(q, k, later_mat, z_mask=None, valid=None):
    z = _dot_nt(q, k)
    if z_mask is not None:
        z = z + z_mask
    if valid is not None:
        z = jnp.where(valid, z, MASK_VALUE)
    log_1m = -(jnp.maximum(z, 0.0) + jnp.log(1.0 + jnp.exp(-jnp.abs(z))))
    hi = log_1m.astype(BF16)
    lo = (log_1m - hi.astype(F32)).astype(BF16)
    later = _dot(hi, later_mat) + _dot(lo, later_mat)
    return z + log_1m + later, jnp.sum(log_1m, axis=-1, keepdims=True)


def _stick_sweep_past(qs, load_block, n_blocks, cs, accs, later_mat):
    heads = len(qs)

    def cond(carry):
        j, cs, _ = carry
        return jnp.logical_and(j >= 0, jnp.max(functools.reduce(jnp.maximum, cs)) > EXP_ZERO_BELOW)

    def body(carry):
        j, cs, accs = carry
        k_of, v_of = load_block(j)
        new_cs, new_accs = [], []
        for hh in range(heads):
            logit, row_sum = _stick_terms(qs[hh], k_of(hh).astype(BF16), later_mat)
            weights = jnp.exp(logit + cs[hh]).astype(BF16)
            new_accs.append(accs[hh] + _dot(weights, v_of(hh).astype(BF16)))
            new_cs.append(cs[hh] + row_sum)
        return j - 1, tuple(new_cs), tuple(new_accs)

    _, _, accs = lax.while_loop(cond, body, (n_blocks - 1, tuple(cs), tuple(accs)))
    return accs


def _stick_heads(q_ref, own, prev, later_own, mask_own, later_mat, prev_valid, load_block, n_blocks,
                 o_ref):
    heads = q_ref.shape[1] // HEAD_DIM
    qs, cs, accs = [], [], []
    for hh in range(heads):
        q = q_ref[:, hh * HEAD_DIM:(hh + 1) * HEAD_DIM]
        logit0, sum0 = _stick_terms(q, own[0](hh).astype(BF16), later_own, z_mask=mask_own)
        logit1, sum1 = _stick_terms(q, prev[0](hh).astype(BF16), later_mat, valid=prev_valid)
        accs.append(_dot(jnp.exp(logit0).astype(BF16), own[1](hh).astype(BF16))
                    + _dot(jnp.exp(logit1 + sum0).astype(BF16), prev[1](hh).astype(BF16)))
        cs.append(sum0 + sum1)
        qs.append(q)
    accs = _stick_sweep_past(qs, load_block, n_blocks, cs, accs, later_mat)
    for hh in range(heads):
        o_ref[:, hh * HEAD_DIM:(hh + 1) * HEAD_DIM] = accs[hh].astype(o_ref.dtype)


def _lane_heads(k_ref, v_ref, rows):
    cols = lambda hh: slice(hh * HEAD_DIM, (hh + 1) * HEAD_DIM)
    return (lambda hh: k_ref[rows, cols(hh)]), (lambda hh: v_ref[rows, cols(hh)])


def _row_heads(k_ref, v_ref, n_pos):
    return (lambda hh: _head_rows(k_ref, 0, n_pos, hh)), (lambda hh: _head_rows(v_ref, 0, n_pos, hh))


def _attn_b_prompt_kernel(q_ref, k_ref, v_ref, u_ref, mask_ref, o_ref):
    qi = pl.program_id(2)
    k_ref = k_ref.at[0]
    v_ref = v_ref.at[0]
    later_mat = u_ref[...]
    block = lambda j: _lane_heads(k_ref, v_ref, pl.ds(pl.multiple_of(j * B_BLOCK, B_BLOCK), B_BLOCK))
    _stick_heads(q_ref, block(qi), block(jnp.maximum(qi - 1, 0)), later_mat, mask_ref[...], later_mat,
                 qi > 0, block, qi - 1, o_ref)


def _attn_b_prompt(q, k, v, batch, seq):
    rows, width = q.shape
    heads = width // HEAD_DIM
    assert heads % B_HEADS == 0
    nq = seq // B_BLOCK
    k3 = k.reshape(batch, seq, width)
    v3 = v.reshape(batch, seq, width)
    lanes = B_HEADS * HEAD_DIM
    blk = (B_BLOCK, lanes)
    kv_spec = pl.BlockSpec((1, seq, lanes), lambda b, h, i: (b, 0, h))
    const = pl.BlockSpec((B_BLOCK, B_BLOCK), lambda b, h, i: (0, 0))
    est = 2 * 2 * seq * lanes * k.dtype.itemsize + B_HEADS * 40 * B_BLOCK * B_BLOCK * 4
    return pl.pallas_call(
        _attn_b_prompt_kernel,
        grid=(batch, heads // B_HEADS, nq),
        in_specs=[pl.BlockSpec(blk, lambda b, h, i: (b * nq + i, h)), kv_spec, kv_spec, const, const],
        out_specs=pl.BlockSpec(blk, lambda b, h, i: (b * nq + i, h)),
        out_shape=jax.ShapeDtypeStruct((rows, width), BF16),
        compiler_params=pltpu.CompilerParams(
            dimension_semantics=("parallel", "parallel", "parallel"),
            vmem_limit_bytes=_vmem_limit(est)),
        name="attn_b_prompt",
    )(q, k3, v3, _later_matrix(B_BLOCK), _strict_causal_mask(B_BLOCK))


def _attn_b_sample_kernel(q_ref, kn_ref, vn_ref, k1_ref, v1_ref, kc_hbm, vc_hbm, un_ref, mask_ref, u_ref,
                          o_ref, kbuf, vbuf, sems, *, n_blocks):
    b = pl.program_id(0)
    block_rows = B_BLOCK * N_HEADS

    def fetch_block(j):
        rows = pl.ds(pl.multiple_of(j * block_rows, block_rows), block_rows)
        copies = (pltpu.make_async_copy(kc_hbm.at[b, rows, :], kbuf, sems.at[0]),
                  pltpu.make_async_copy(vc_hbm.at[b, rows, :], vbuf, sems.at[1]))
        for cp in copies:
            cp.start()
        for cp in copies:
            cp.wait()
        return _row_heads(kbuf, vbuf, B_BLOCK)

    _stick_heads(q_ref, _lane_heads(kn_ref, vn_ref, slice(None)),
                 _row_heads(k1_ref.at[0], v1_ref.at[0], B_BLOCK),
                 un_ref[...], mask_ref[...], u_ref[...], None, fetch_block, n_blocks - 1, o_ref)


def _attn_b_sample(q, k, v, cache_k, cache_v, batch, n):
    rows, width = q.shape
    lc = cache_k.shape[1] // N_HEADS
    assert lc % B_BLOCK == 0 and lc >= B_BLOCK and width == N_HEADS * HEAD_DIM
    n_blocks = lc // B_BLOCK
    block_rows = B_BLOCK * N_HEADS
    blk = pl.BlockSpec((n, width), lambda b: (b, 0))
    newest = pl.BlockSpec((1, block_rows, HEAD_DIM), lambda b: (b, n_blocks - 1, 0))
    anywhere = pl.BlockSpec(memory_space=pl.ANY)
    const = lambda shape: pl.BlockSpec(shape, lambda b: (0, 0))
    est = 6 * block_rows * HEAD_DIM * 4 + N_HEADS * 40 * n * B_BLOCK * 4 + 8 * B_BLOCK * B_BLOCK
    return pl.pallas_call(
        functools.partial(_attn_b_sample_kernel, n_blocks=n_blocks),
        grid=(batch,),
        in_specs=[blk, blk, blk, newest, newest, anywhere, anywhere,
                  const((n, n)), const((n, n)), const((B_BLOCK, B_BLOCK))],
        out_specs=blk,
        out_shape=jax.ShapeDtypeStruct((rows, width), BF16),
        scratch_shapes=[pltpu.VMEM((block_rows, HEAD_DIM), F32), pltpu.VMEM((block_rows, HEAD_DIM), F32),
                        pltpu.SemaphoreType.DMA((2,))],
        compiler_params=pltpu.CompilerParams(
            dimension_semantics=("arbitrary",), vmem_limit_bytes=_vmem_limit(est)),
        name="attn_b_sample",
    )(q, k, v, cache_k, cache_v, cache_k, cache_v,
      _later_matrix(n), _strict_causal_mask(n), _later_matrix(B_BLOCK))


MIX_COLS = 512


def _mix_out_kernel(x_ref, oa_ref, ob_ref, gates_ref, gt_ref, wa_ref, wb_ref, wo_ref, o_ref, y_ref,
                    *, groups, ts):
    d = o_ref.shape[-1]
    oa = oa_ref[...]
    ob = ob_ref[...]
    for c0 in range(0, d, MIX_COLS):
        cols = slice(c0, c0 + MIX_COLS)
        ya = _dot(oa, wa_ref[:, cols])
        yb = _dot(ob, wb_ref[:, cols])
        ga = gates_ref[:, cols].astype(F32)
        gb = gates_ref[:, d + c0:d + c0 + MIX_COLS].astype(F32)
        y_ref[:, cols] = (_sigmoid(ga) * ya + _sigmoid(gb) * yb).astype(BF16)
    y = y_ref[...]
    for c0 in range(0, d, MIX_COLS):
        cols = slice(c0, c0 + MIX_COLS)
        merged = _dot(y, wo_ref[:, cols])
        for gi in range(groups):
            rows = slice(gi * ts, (gi + 1) * ts)
            o_ref[gi, :, cols] = x_ref[gi, :, cols] + gt_ref[gi, :, cols] * merged[rows]


def _mix_out(x, oa, ob, gates, gt, wa, wb, wo, tile_rows):
    batch, seq, d = x.shape
    width = oa.shape[1]
    groups, ts, n_tiles, x_map, b_map = _row_tiling(batch, seq, tile_rows)
    tm = groups * ts
    const = lambda i: (0, 0)
    single = pl.Buffered(1)
    est = (2 * 2 * tm * d * 4 + 2 * 2 * tm * width * 2 + 2 * tm * 2 * d * 2 + tm * d * 2
           + (2 * width * d + d * d) * 2 + 6 * tm * MIX_COLS * 4)
    return pl.pallas_call(
        functools.partial(_mix_out_kernel, groups=groups, ts=ts),
        grid=(n_tiles,),
        in_specs=[
            pl.BlockSpec((groups, ts, d), x_map),
            pl.BlockSpec((tm, width), lambda i: (i, 0)),
            pl.BlockSpec((tm, width), lambda i: (i, 0)),
            pl.BlockSpec((tm, 2 * d), lambda i: (i, 0)),
            pl.BlockSpec((groups, 1, d), b_map),
            pl.BlockSpec((width, d), const, pipeline_mode=single),
            pl.BlockSpec((width, d), const, pipeline_mode=single),
            pl.BlockSpec((d, d), const, pipeline_mode=single),
        ],
        out_specs=pl.BlockSpec((groups, ts, d), x_map),
        out_shape=jax.ShapeDtypeStruct((batch, seq, d), F32),
        scratch_shapes=[pltpu.VMEM((tm, d), BF16)],
        compiler_params=pltpu.CompilerParams(
            dimension_semantics=("parallel",), vmem_limit_bytes=_vmem_limit(est)),
        name="mix_out",
    )(x, oa, ob, gates, gt, wa, wb, wo)


FF_TILE = 512
FF_OUT_COLS = 512


def _ffn_kernel(x_hbm, g_ref, sc_ref, sh_ref, gt_ref, gf_ref, wg_ref, wu_ref, wd_ref, o_ref,
                x_buf, h_ref, x_sem, *, groups, ts, tiles_per_batch):
    i = pl.program_id(0)
    f = pl.program_id(1)

    def x_copy(tile):
        if groups == 1:
            src = x_hbm.at[pl.ds(tile // tiles_per_batch, 1),
                           pl.ds(pl.multiple_of((tile % tiles_per_batch) * ts, ts), ts), :]
        else:
            src = x_hbm.at[pl.ds(pl.multiple_of(tile * groups, groups), groups), :, :]
        return pltpu.make_async_copy(src, x_buf, x_sem.at[0])

    @pl.when(f == 0)
    def _():
        @pl.when(i == 0)
        def _():
            x_copy(0).start()

        x_copy(i).wait()
        _norm_mod_rows(x_buf, g_ref, sc_ref, sh_ref, h_ref, groups, ts, copy_ref=o_ref)

    @pl.when(jnp.logical_and(f == 1, i + 1 < pl.num_programs(0)))
    def _():
        x_copy(i + 1).start()

    h = h_ref[...]
    gate = _dot(h, wg_ref[...])
    up = _dot(h, wu_ref[...])
    hidden = (gate * _sigmoid(gate) * up).astype(BF16)
    d = o_ref.shape[-1]
    for c0 in range(0, d, FF_OUT_COLS):
        cols = slice(c0, c0 + FF_OUT_COLS)
        part = _dot(hidden, wd_ref[:, cols])
        for gi in range(groups):
            o_ref[gi, :, cols] += gt_ref[gi, :, cols] * part[gi * ts:(gi + 1) * ts]

    @pl.when(f == pl.num_programs(1) - 1)
    def _():
        gf = gf_ref[...]
        for gi in range(groups):
            for r0 in range(0, ts, NORM_ROWS):
                rows = slice(r0, r0 + NORM_ROWS)
                x2 = o_ref[gi, rows, :]
                ms = jnp.mean(x2 * x2, axis=-1, keepdims=True)
                o_ref[gi, rows, :] = x2 * lax.rsqrt(ms + EPS) * gf


def _ffn(x, g, sc, sh, gt, g_final, w_gate_up, w_down, tile_rows):
    batch, seq, d = x.shape
    d_ff = w_down.shape[0]
    nf = d_ff // FF_TILE
    groups, ts, n_tiles, x_map, b_map = _row_tiling(batch, seq, tile_rows)
    tm = groups * ts
    xm = lambda i, f: x_map(i)
    bm = lambda i, f: b_map(i)
    vec = pl.BlockSpec((1, d), lambda i, f: (0, 0))
    mod = pl.BlockSpec((groups, 1, d), bm)
    assert nf >= 2
    est = 3 * tm * d * 4 + tm * d * 2 + 2 * 3 * d * FF_TILE * 2 + 5 * tm * FF_TILE * 4
    return pl.pallas_call(
        functools.partial(_ffn_kernel, groups=groups, ts=ts, tiles_per_batch=max(seq // ts, 1)),
        grid=(n_tiles, nf),
        in_specs=[
            pl.BlockSpec(memory_space=pl.ANY), vec, mod, mod, mod, vec,
            pl.BlockSpec((d, FF_TILE), lambda i, f: (0, f)),
            pl.BlockSpec((d, FF_TILE), lambda i, f: (0, nf + f)),
            pl.BlockSpec((FF_TILE, d), lambda i, f: (f, 0)),
        ],
        out_specs=pl.BlockSpec((groups, ts, d), xm),
        out_shape=jax.ShapeDtypeStruct((batch, seq, d), F32),
        scratch_shapes=[pltpu.VMEM((groups, ts, d), F32), pltpu.VMEM((tm, d), BF16),
                        pltpu.SemaphoreType.DMA((1,))],
        compiler_params=pltpu.CompilerParams(
            dimension_semantics=("arbitrary", "arbitrary"), vmem_limit_bytes=_vmem_limit(est)),
        name="ffn",
    )(x, g.reshape(1, d), sc, sh, gt, g_final.reshape(1, d), w_gate_up, w_gate_up, w_down)


def _layer(x, mods, w, attn_a, attn_b, g_final, in_rows, mix_rows, ffn_rows):
    batch, seq, d = x.shape
    sh1, sc1, gt1, sh2, sc2, gt2 = mods
    width = w["w_a_out"].shape[0]
    qa, qb, ka, va, kb, vb, ka4, va4, kb4, vb4, gates = _inproj(
        x, w["g_mix"], sc1, sh1, w["w_in"], in_rows, width)
    oa = attn_a(qa, ka, va)
    ob = attn_b(qb, kb, vb)
    x1 = _mix_out(x, oa, ob, gates, gt1, w["w_a_out"], w["w_b_out"], w["w_o"], mix_rows)
    y = _ffn(x1, w["g_ffn"], sc2, sh2, gt2, g_final, w["w_gate_up"], w["w_down"], ffn_rows)
    heads_of = lambda t: t.reshape(batch, seq, N_HEADS, HEAD_DIM)
    return y, heads_of(ka4), heads_of(va4), heads_of(kb4), heads_of(vb4)


def kernel(x_prompt, x_sample, c_prompt, c_sample, cache_a_k, cache_a_v, cache_b_k, cache_b_v,
           w_ada, b_ada, g_mix, w_in, rel_bias, w_a_out, w_b_out, w_o, g_ffn, w_gate_up, w_down,
           g_final):
    depth = w_in.shape[0]
    assert depth == 1
    batch, seq, d = x_prompt.shape
    dec_batch, dec_seq, _ = x_sample.shape
    past_len = cache_b_k.shape[2]
    l = 0

    n_c = batch + dec_batch
    c_rows = -(-n_c // 8) * 8
    c_all = jnp.concatenate([c_prompt, c_sample, jnp.zeros((c_rows - n_c, d), F32)], axis=0)
    mod_all = _ada(c_all, w_ada[l], b_ada[l])

    def mods_of(rows):
        return tuple(m[:, None, :] for m in jnp.split(rows, 6, axis=-1))

    mods_p = mods_of(mod_all[:batch])
    mods_s = mods_of(mod_all[batch:n_c])

    w = {
        "g_mix": g_mix[l], "g_ffn": g_ffn[l],
        "w_in": w_in[l].astype(BF16), "w_a_out": w_a_out[l].astype(BF16),
        "w_b_out": w_b_out[l].astype(BF16), "w_o": w_o[l].astype(BF16),
        "w_gate_up": w_gate_up[l].astype(BF16), "w_down": w_down[l].astype(BF16),
    }

    yp, ka, va, kb, vb = _layer(
        x_prompt, mods_p, w,
        functools.partial(_attn_a_prompt, rel_bias=rel_bias[l], batch=batch, seq=seq),
        functools.partial(_attn_b_prompt, batch=batch, seq=seq),
        g_final, in_rows=512, mix_rows=256, ffn_rows=1024)
    keep = min(BAND, seq)
    outs_p = (ka[:, -keep:], va[:, -keep:], kb, vb)

    flat_rows = lambda c: c[l].reshape(dec_batch, c.shape[2] * c.shape[3], c.shape[4])
    cak, cav, cbk, cbv = (flat_rows(c) for c in (cache_a_k, cache_a_v, cache_b_k, cache_b_v))
    ys, ka_s, va_s, kb_s, vb_s = _layer(
        x_sample, mods_s, w,
        functools.partial(_attn_a_sample, cache_k=cak, cache_v=cav, rel_bias=rel_bias[l],
                          batch=dec_batch, n=dec_seq, past_len=past_len),
        functools.partial(_attn_b_sample, cache_k=cbk, cache_v=cbv, batch=dec_batch, n=dec_seq),
        g_final, in_rows=dec_batch * dec_seq, mix_rows=dec_batch * dec_seq,
        ffn_rows=dec_batch * dec_seq)
    outs_s = (ka_s, va_s, kb_s, vb_s)

    return (yp, ys) + tuple(t[None] for t in outs_p) + tuple(t[None] for t in outs_s)
```

```python
import functools

import jax
import jax.numpy as jnp
from jax import lax
from jax.experimental import pallas as pl
from jax.experimental.pallas import tpu as pltpu

F32 = jnp.float32
BF16 = jnp.bfloat16

HEAD_DIM = 128
CHUNK = 64
BAND_CHUNKS = 8
BAND = BAND_CHUNKS * CHUNK
REL_CLIP = 128
EPS = 1e-6
ATTN_SCALE = HEAD_DIM ** -0.5
LOG2_E = 1.4426950408889634
MASK_VALUE = -1e30

V7X_VMEM_BYTES = 64 * 1024 * 1024
V7X_LANES = 128
V7X_BF16_SUBLANES = 16

EXP_ZERO_BELOW = -104.0

NORM_ROWS = 16
NORM_UNROLL = 8


def _norm_unroll(n_rows):
    steps = n_rows // NORM_ROWS
    return NORM_UNROLL if steps % NORM_UNROLL == 0 else 1


def _vmem_limit(estimate_bytes):
    return int(min(V7X_VMEM_BYTES - (4 << 20), max(estimate_bytes * 5 // 4, 16 << 20)))


def _dot(a, b):
    return jnp.dot(a, b, preferred_element_type=F32)


def _dot_nt(a, b):
    return lax.dot_general(a, b, (((1,), (1,)), ((), ())), preferred_element_type=F32)


def _sigmoid(x):
    return 1.0 / (1.0 + jnp.exp(-x))


def _ada_kernel(c_ref, w_ref, b_ref, o_ref):
    c = c_ref[...]
    a = (c * _sigmoid(c)).astype(BF16)
    o_ref[...] = _dot(a, w_ref[...].astype(BF16)) + b_ref[...]


def _ada(c_all, w_ada, b_ada):
    rows, d = c_all.shape
    n = w_ada.shape[1]
    tn = 1024
    est = 2 * d * tn * 4 + d * tn * 2 + 4 * rows * (d + tn) * 4
    return pl.pallas_call(
        _ada_kernel,
        grid=(n // tn,),
        in_specs=[
            pl.BlockSpec((rows, d), lambda j: (0, 0)),
            pl.BlockSpec((d, tn), lambda j: (0, j)),
            pl.BlockSpec((1, tn), lambda j: (0, j)),
        ],
        out_specs=pl.BlockSpec((rows, tn), lambda j: (0, j)),
        out_shape=jax.ShapeDtypeStruct((rows, n), F32),
        compiler_params=pltpu.CompilerParams(
            dimension_semantics=("parallel",), vmem_limit_bytes=_vmem_limit(est)),
        name="ada",
    )(c_all, w_ada, b_ada.reshape(1, n))


def _norm_mod_rows(x_ref, g_ref, sc_ref, sh_ref, h_ref, groups, ts, copy_ref=None):
    gvec = g_ref[...]
    for gi in range(groups):
        scale = gvec * (1.0 + sc_ref[gi])
        shift = sh_ref[gi]

        def body(r, carry, gi=gi, scale=scale, shift=shift):
            r0 = pl.multiple_of(r * NORM_ROWS, NORM_ROWS)
            x = x_ref[gi, pl.ds(r0, NORM_ROWS), :]
            if copy_ref is not None:
                copy_ref[gi, pl.ds(r0, NORM_ROWS), :] = x
            ms = jnp.mean(x * x, axis=-1, keepdims=True)
            h = x * lax.rsqrt(ms + EPS) * scale + shift
            h_ref[pl.ds(pl.multiple_of(gi * ts + r0, NORM_ROWS), NORM_ROWS), :] = h.astype(BF16)
            return carry

        lax.fori_loop(0, ts // NORM_ROWS, body, 0, unroll=_norm_unroll(ts))


def _row_tiling(batch, seq, tile_rows):
    if seq >= tile_rows:
        assert seq % tile_rows == 0
        groups, ts, ns = 1, tile_rows, seq // tile_rows
    else:
        assert tile_rows % seq == 0 and batch % (tile_rows // seq) == 0
        groups, ts, ns = tile_rows // seq, seq, 1
    n_tiles = batch * seq // (groups * ts)
    if groups == 1:
        x_map = lambda i: (i // ns, i % ns, 0)
        b_map = lambda i: (i // ns, 0, 0)
    else:
        x_map = lambda i: (i, 0, 0)
        b_map = lambda i: (i, 0, 0)
    return groups, ts, n_tiles, x_map, b_map


N_QKV = 6
N_HEADS = 8


N_STAGE = 2


def _inproj_kernel(x_ref, g_ref, sc_ref, sh_ref, w_ref,
                   qa_hbm, qb_hbm, ka_hbm, va_hbm, kb_hbm, vb_hbm,
                   ka4_hbm, va4_hbm, kb4_hbm, vb4_hbm, gates_hbm,
                   h_ref, stage16, stage32, sem16, sem32, *, groups, ts):
    i = pl.program_id(0)
    j = pl.program_id(1)
    tm = groups * ts
    width = w_ref.shape[1]
    rows = pl.ds(pl.multiple_of(i * tm, tm), tm)
    rows4 = pl.ds(pl.multiple_of(i * (tm * N_HEADS), tm * N_HEADS), tm * N_HEADS)
    slot16 = j % N_STAGE
    kv_slot = {1: 0, 2: 1, 4: 0, 5: 1}

    def copy16(slot, dst):
        return pltpu.make_async_copy(stage16.at[slot], dst, sem16.at[slot])

    def copy32(slot, dst_hbm):
        return pltpu.make_async_copy(stage32.at[slot], dst_hbm.at[rows4, :], sem32.at[slot])

    def free16():
        @pl.when(jnp.logical_or(i > 0, j >= N_STAGE))
        def _():
            copy16(slot16, qa_hbm.at[rows, :]).wait()

    @pl.when(j == 0)
    def _():
        _norm_mod_rows(x_ref, g_ref, sc_ref, sh_ref, h_ref, groups, ts)

    for idx, q_hbm, scale in ((0, qa_hbm, ATTN_SCALE * LOG2_E), (3, qb_hbm, ATTN_SCALE)):
        @pl.when(j == idx)
        def _(q_hbm=q_hbm, scale=scale):
            free16()
            stage16[slot16] = (_dot(h_ref[...], w_ref[...]) * scale).astype(BF16)
            copy16(slot16, q_hbm.at[rows, :]).start()

    for idx, kv_hbm, kv4_hbm in ((1, ka_hbm, ka4_hbm), (2, va_hbm, va4_hbm),
                                 (4, kb_hbm, kb4_hbm), (5, vb_hbm, vb4_hbm)):
        @pl.when(j == idx)
        def _(idx=idx, kv_hbm=kv_hbm, kv4_hbm=kv4_hbm):
            slot32 = kv_slot[idx]
            free16()
            if idx in (4, 5):
                copy32(slot32, kv4_hbm).wait()
            else:
                @pl.when(i > 0)
                def _():
                    copy32(slot32, kv4_hbm).wait()
            p = _dot(h_ref[...], w_ref[...])
            stage16[slot16] = p.astype(BF16)
            copy16(slot16, kv_hbm.at[rows, :]).start()
            for hd in range(N_HEADS):
                stage32[slot32, pl.ds(hd, tm, stride=N_HEADS), :] = p[:, hd * HEAD_DIM:(hd + 1) * HEAD_DIM]
            copy32(slot32, kv4_hbm).start()

    @pl.when(j >= N_QKV)
    def _():
        free16()
        stage16[slot16] = _dot(h_ref[...], w_ref[...]).astype(BF16)
        col0 = pl.multiple_of((j - N_QKV) * width, width)
        copy16(slot16, gates_hbm.at[rows, pl.ds(col0, width)]).start()

        @pl.when(jnp.logical_and(i == pl.num_programs(0) - 1, j == pl.num_programs(1) - 1))
        def _():
            for slot in range(N_STAGE):
                copy16(slot, qa_hbm.at[rows, :]).wait()
                copy32(slot, ka4_hbm).wait()


def _inproj(x, g, sc, sh, w_in, tile_rows, width):
    batch, seq, d = x.shape
    rows = batch * seq
    n_cols = w_in.shape[1]
    assert width == N_HEADS * HEAD_DIM and n_cols % width == 0
    nj = n_cols // width
    n_gate_tiles = nj - N_QKV
    assert nj % N_STAGE == 0 and n_gate_tiles >= N_STAGE
    groups, ts, n_tiles, x_map, b_map = _row_tiling(batch, seq, tile_rows)
    tm = groups * ts
    xm = lambda i, j: x_map(i)
    bm = lambda i, j: b_map(i)
    anywhere = pl.BlockSpec(memory_space=pl.ANY)
    est = (2 * tm * d * 4 + 2 * d * width * 2 + tm * d * 2
           + N_STAGE * tm * width * (2 + 4) + 2 * tm * width * 4)
    bf16_rows = jax.ShapeDtypeStruct((rows, width), BF16)
    f32_flat = jax.ShapeDtypeStruct((rows * N_HEADS, HEAD_DIM), F32)
    return pl.pallas_call(
        functools.partial(_inproj_kernel, groups=groups, ts=ts),
        grid=(n_tiles, nj),
        in_specs=[
            pl.BlockSpec((groups, ts, d), xm),
            pl.BlockSpec((1, d), lambda i, j: (0, 0)),
            pl.BlockSpec((groups, 1, d), bm),
            pl.BlockSpec((groups, 1, d), bm),
            pl.BlockSpec((d, width), lambda i, j: (0, j)),
        ],
        out_specs=[anywhere] * 11,
        out_shape=[bf16_rows] * 6 + [f32_flat] * 4
        + [jax.ShapeDtypeStruct((rows, n_gate_tiles * width), BF16)],
        scratch_shapes=[pltpu.VMEM((tm, d), BF16),
                        pltpu.VMEM((N_STAGE, tm, width), BF16),
                        pltpu.VMEM((N_STAGE, tm * N_HEADS, HEAD_DIM), F32),
                        pltpu.SemaphoreType.DMA((N_STAGE,)), pltpu.SemaphoreType.DMA((N_STAGE,))],
        compiler_params=pltpu.CompilerParams(
            dimension_semantics=("arbitrary", "arbitrary"), vmem_limit_bytes=_vmem_limit(est)),
        name="inproj",
    )(x, g.reshape(1, d), sc, sh, w_in)


A_BLOCK = 256


A_HEADS = 8
A_VARIANTS = 3
TOEPLITZ_ROW = 1024


def _toeplitz_row(rel_bias, offset, n_cols):
    m = jnp.arange(TOEPLITZ_ROW)
    m = jnp.where(m < n_cols, m, m - TOEPLITZ_ROW)
    idx = jnp.clip(offset - m, -REL_CLIP, REL_CLIP) + REL_CLIP
    return rel_bias.astype(F32)[:, None, idx]


def _toeplitz_bias(row, n_rows):
    return pltpu.roll(jnp.broadcast_to(row, (n_rows, TOEPLITZ_ROW)), 0, 1, stride=1, stride_axis=0)


def _attn_a_prompt_kernel(g_ref, q_ref, k0_ref, k1_ref, k2_ref, v0_ref, v1_ref, v2_ref, o_ref, bias_ref):
    qi = pl.program_id(2)
    n_keys = 3 * A_BLOCK

    @pl.when(qi == 0)
    def _():
        qc = lax.broadcasted_iota(jnp.int32, (A_BLOCK, n_keys), 0) // CHUNK
        col = lax.broadcasted_iota(jnp.int32, (A_BLOCK, n_keys), 1)
        kc = col // CHUNK - BAND // CHUNK
        in_band = jnp.logical_and(kc <= qc, kc >= qc - BAND_CHUNKS)
        for hh in range(A_HEADS):
            rel = _toeplitz_bias(g_ref[hh], A_BLOCK)[:, :n_keys] * LOG2_E
            full = jnp.where(in_band, rel, MASK_VALUE)
            for v in range(A_VARIANTS):
                bias_ref[v, hh] = jnp.where(col >= (A_VARIANTS - 1 - v) * A_BLOCK, full, MASK_VALUE)

    variant = jnp.minimum(qi, A_VARIANTS - 1)
    for hh in range(A_HEADS):
        cols = slice(hh * HEAD_DIM, (hh + 1) * HEAD_DIM)
        q = q_ref[:, cols]
        scores = [
            _dot_nt(q, k_ref[:, cols].astype(BF16))
            + bias_ref[variant, hh, :, jb * A_BLOCK:(jb + 1) * A_BLOCK]
            for jb, k_ref in enumerate((k0_ref, k1_ref, k2_ref))]
        m = jnp.max(functools.reduce(jnp.maximum, scores), axis=-1, keepdims=True)
        es = [jnp.exp2(s - m) for s in scores]
        l = jnp.sum(functools.reduce(jnp.add, es), axis=-1, keepdims=True)
        acc = functools.reduce(jnp.add, [
            _dot(e.astype(BF16), v_ref[:, cols].astype(BF16))
            for e, v_ref in zip(es, (v0_ref, v1_ref, v2_ref))])
        o_ref[:, cols] = (acc / l).astype(o_ref.dtype)


def _attn_a_prompt(q, k, v, rel_bias, batch, seq):
    rows, width = q.shape
    heads = width // HEAD_DIM
    nq = seq // A_BLOCK
    assert BAND == 2 * A_BLOCK and seq % A_BLOCK == 0 and heads % A_HEADS == 0
    assert 4 * A_BLOCK - 1 <= TOEPLITZ_ROW
    g = _toeplitz_row(rel_bias, BAND, 3 * A_BLOCK)
    blk = (A_BLOCK, A_HEADS * HEAD_DIM)

    def kv_spec(jb):
        return pl.BlockSpec(blk, lambda b, h, i: (b * nq + jnp.maximum(i + (jb - 2), 0), h))

    est = (2 * A_BLOCK * A_HEADS * HEAD_DIM * (2 + 6 * 4 + 2)
           + A_HEADS * (A_VARIANTS * A_BLOCK * 3 * A_BLOCK * 4 + 8 * A_BLOCK * 3 * A_BLOCK * 4))
    return pl.pallas_call(
        _attn_a_prompt_kernel,
        grid=(batch, heads // A_HEADS, nq),
        in_specs=[pl.BlockSpec((A_HEADS, 1, TOEPLITZ_ROW), lambda b, h, i: (h, 0, 0)),
                  pl.BlockSpec(blk, lambda b, h, i: (b * nq + i, h)),
                  kv_spec(0), kv_spec(1), kv_spec(2), kv_spec(0), kv_spec(1), kv_spec(2)],
        out_specs=pl.BlockSpec(blk, lambda b, h, i: (b * nq + i, h)),
        out_shape=jax.ShapeDtypeStruct((rows, width), BF16),
        scratch_shapes=[pltpu.VMEM((A_VARIANTS, A_HEADS, A_BLOCK, 3 * A_BLOCK), F32)],
        compiler_params=pltpu.CompilerParams(
            dimension_semantics=("parallel", "parallel", "arbitrary"),
            vmem_limit_bytes=_vmem_limit(est)),
        name="attn_a_prompt",
    )(g, q, k, k, k, v, v, v)


def _head_rows(ref, first_pos, n_pos, head):
    return ref[pl.ds(first_pos * N_HEADS + head, n_pos, stride=N_HEADS), :]


def _attn_a_sample_kernel(g_ref, q_ref, kc_ref, vc_ref, kn_ref, vn_ref, o_ref, *, lc):
    n = q_ref.shape[0]
    kc_ref = kc_ref.at[0]
    vc_ref = vc_ref.at[0]
    for hh in range(N_HEADS):
        cols = slice(hh * HEAD_DIM, (hh + 1) * HEAD_DIM)
        q = q_ref[:, cols]
        bias = _toeplitz_bias(g_ref[hh], n) * LOG2_E
        sc = _dot_nt(q, _head_rows(kc_ref, 0, lc, hh).astype(BF16)) + bias[:, :lc]
        sn = _dot_nt(q, kn_ref[:, cols]) + bias[:, lc:lc + n]
        m = jnp.maximum(jnp.max(sc, axis=-1, keepdims=True), jnp.max(sn, axis=-1, keepdims=True))
        ec = jnp.exp2(sc - m)
        en = jnp.exp2(sn - m)
        l = jnp.sum(ec, axis=-1, keepdims=True) + jnp.sum(en, axis=-1, keepdims=True)
        acc = (_dot(ec.astype(BF16), _head_rows(vc_ref, 0, lc, hh).astype(BF16))
               + _dot(en.astype(BF16), vn_ref[:, cols]))
        o_ref[:, cols] = (acc / l).astype(o_ref.dtype)


def _attn_a_sample(q, k, v, cache_k, cache_v, rel_bias, batch, n, past_len):
    rows, width = q.shape
    lc = cache_k.shape[1] // N_HEADS
    assert past_len >= lc and lc + 2 * n - 1 <= TOEPLITZ_ROW and lc % V7X_LANES == 0
    assert width == N_HEADS * HEAD_DIM
    g = _toeplitz_row(rel_bias, lc, lc + n)
    blk = pl.BlockSpec((n, width), lambda b: (b, 0))
    cblk = pl.BlockSpec((1, lc * N_HEADS, HEAD_DIM), lambda b: (b, 0, 0))
    est = 2 * 2 * lc * width * 4 + 16 * n * lc * 4 * N_HEADS
    return pl.pallas_call(
        functools.partial(_attn_a_sample_kernel, lc=lc),
        grid=(batch,),
        in_specs=[pl.BlockSpec((N_HEADS, 1, TOEPLITZ_ROW), lambda b: (0, 0, 0)),
                  blk, cblk, cblk, blk, blk],
        out_specs=blk,
        out_shape=jax.ShapeDtypeStruct((rows, width), BF16),
        compiler_params=pltpu.CompilerParams(
            dimension_semantics=("parallel",), vmem_limit_bytes=_vmem_limit(est)),
        name="attn_a_sample",
    )(g, q, cache_k, cache_v, k, v)


B_BLOCK = 256
B_HEADS = 4


def _later_matrix(n):
    j = jnp.arange(n)[:, None]
    s = jnp.arange(n)[None, :]
    return (j > s).astype(BF16)


def _strict_causal_mask(n):
    t = jnp.arange(n)[:, None]
    s = jnp.arange(n)[None, :]
    return jnp.where(s < t, 0.0, MASK_VALUE).astype(F32)


def _stick_terms(q, k, later_mat, z_mask=None, valid=None):
    z = _dot_nt(q, k)
    if z_mask is not None:
        z = z + z_mask
    if valid is not None:
        z = jnp.where(valid, z, MASK_VALUE)
    log_1m = -(jnp.maximum(z, 0.0) + jnp.log(1.0 + jnp.exp(-jnp.abs(z))))
    hi = log_1m.astype(BF16)
    lo = (log_1m - hi.astype(F32)).astype(BF16)
    later = _dot(hi, later_mat) + _dot(lo, later_mat)
    return z + log_1m + later, jnp.sum(log_1m, axis=-1, keepdims=True)


def _stick_sweep_past(qs, load_block, n_blocks, cs, accs, later_mat):
    heads = len(qs)

    def cond(carry):
        j, cs, _ = carry
        return jnp.logical_and(j >= 0, jnp.max(functools.reduce(jnp.maximum, cs)) > EXP_ZERO_BELOW)

    def body(carry):
        j, cs, accs = carry
        k_of, v_of = load_block(j)
        new_cs, new_accs = [], []
        for hh in range(heads):
            logit, row_sum = _stick_terms(qs[hh], k_of(hh).astype(BF16), later_mat)
            weights = jnp.exp(logit + cs[hh]).astype(BF16)
            new_accs.append(accs[hh] + _dot(weights, v_of(hh).astype(BF16)))
            new_cs.append(cs[hh] + row_sum)
        return j - 1, tuple(new_cs), tuple(new_accs)

    _, _, accs = lax.while_loop(cond, body, (n_blocks - 1, tuple(cs), tuple(accs)))
    return accs


def _stick_heads(q_ref, own, prev, later_own, mask_own, later_mat, prev_valid, load_block, n_blocks,
                 o_ref):
    heads = q_ref.shape[1] // HEAD_DIM
    qs, cs, accs = [], [], []
    for hh in range(heads):
        q = q_ref[:, hh * HEAD_DIM:(hh + 1) * HEAD_DIM]
        logit0, sum0 = _stick_terms(q, own[0](hh).astype(BF16), later_own, z_mask=mask_own)
        logit1, sum1 = _stick_terms(q, prev[0](hh).astype(BF16), later_mat, valid=prev_valid)
        accs.append(_dot(jnp.exp(logit0).astype(BF16), own[1](hh).astype(BF16))
                    + _dot(jnp.exp(logit1 + sum0).astype(BF16), prev[1](hh).astype(BF16)))
        cs.append(sum0 + sum1)
        qs.append(q)
    accs = _stick_sweep_past(qs, load_block, n_blocks, cs, accs, later_mat)
    for hh in range(heads):
        o_ref[:, hh * HEAD_DIM:(hh + 1) * HEAD_DIM] = accs[hh].astype(o_ref.dtype)


def _lane_heads(k_ref, v_ref, rows):
    cols = lambda hh: slice(hh * HEAD_DIM, (hh + 1) * HEAD_DIM)
    return (lambda hh: k_ref[rows, cols(hh)]), (lambda hh: v_ref[rows, cols(hh)])


def _row_heads(k_ref, v_ref, n_pos):
    return (lambda hh: _head_rows(k_ref, 0, n_pos, hh)), (lambda hh: _head_rows(v_ref, 0, n_pos, hh))


def _attn_b_prompt_kernel(q_ref, k_ref, v_ref, u_ref, mask_ref, o_ref):
    qi = pl.program_id(2)
    k_ref = k_ref.at[0]
    v_ref = v_ref.at[0]
    later_mat = u_ref[...]
    block = lambda j: _lane_heads(k_ref, v_ref, pl.ds(pl.multiple_of(j * B_BLOCK, B_BLOCK), B_BLOCK))
    _stick_heads(q_ref, block(qi), block(jnp.maximum(qi - 1, 0)), later_mat, mask_ref[...], later_mat,
                 qi > 0, block, qi - 1, o_ref)


def _attn_b_prompt(q, k, v, batch, seq):
    rows, width = q.shape
    heads = width // HEAD_DIM
    assert heads % B_HEADS == 0
    nq = seq // B_BLOCK
    k3 = k.reshape(batch, seq, width)
    v3 = v.reshape(batch, seq, width)
    lanes = B_HEADS * HEAD_DIM
    blk = (B_BLOCK, lanes)
    kv_spec = pl.BlockSpec((1, seq, lanes), lambda b, h, i: (b, 0, h))
    const = pl.BlockSpec((B_BLOCK, B_BLOCK), lambda b, h, i: (0, 0))
    est = 2 * 2 * seq * lanes * k.dtype.itemsize + B_HEADS * 40 * B_BLOCK * B_BLOCK * 4
    return pl.pallas_call(
        _attn_b_prompt_kernel,
        grid=(batch, heads // B_HEADS, nq),
        in_specs=[pl.BlockSpec(blk, lambda b, h, i: (b * nq + i, h)), kv_spec, kv_spec, const, const],
        out_specs=pl.BlockSpec(blk, lambda b, h, i: (b * nq + i, h)),
        out_shape=jax.ShapeDtypeStruct((rows, width), BF16),
        compiler_params=pltpu.CompilerParams(
            dimension_semantics=("parallel", "parallel", "parallel"),
            vmem_limit_bytes=_vmem_limit(est)),
        name="attn_b_prompt",
    )(q, k3, v3, _later_matrix(B_BLOCK), _strict_causal_mask(B_BLOCK))


def _attn_b_sample_kernel(q_ref, kn_ref, vn_ref, k1_ref, v1_ref, kc_hbm, vc_hbm, un_ref, mask_ref, u_ref,
                          o_ref, kbuf, vbuf, sems, *, n_blocks):
    b = pl.program_id(0)
    block_rows = B_BLOCK * N_HEADS

    def fetch_block(j):
        rows = pl.ds(pl.multiple_of(j * block_rows, block_rows), block_rows)
        copies = (pltpu.make_async_copy(kc_hbm.at[b, rows, :], kbuf, sems.at[0]),
                  pltpu.make_async_copy(vc_hbm.at[b, rows, :], vbuf, sems.at[1]))
        for cp in copies:
            cp.start()
        for cp in copies:
            cp.wait()
        return _row_heads(kbuf, vbuf, B_BLOCK)

    _stick_heads(q_ref, _lane_heads(kn_ref, vn_ref, slice(None)),
                 _row_heads(k1_ref.at[0], v1_ref.at[0], B_BLOCK),
                 un_ref[...], mask_ref[...], u_ref[...], None, fetch_block, n_blocks - 1, o_ref)


def _attn_b_sample(q, k, v, cache_k, cache_v, batch, n):
    rows, width = q.shape
    lc = cache_k.shape[1] // N_HEADS
    assert lc % B_BLOCK == 0 and lc >= B_BLOCK and width == N_HEADS * HEAD_DIM
    n_blocks = lc // B_BLOCK
    block_rows = B_BLOCK * N_HEADS
    blk = pl.BlockSpec((n, width), lambda b: (b, 0))
    newest = pl.BlockSpec((1, block_rows, HEAD_DIM), lambda b: (b, n_blocks - 1, 0))
    anywhere = pl.BlockSpec(memory_space=pl.ANY)
    const = lambda shape: pl.BlockSpec(shape, lambda b: (0, 0))
    est = 6 * block_rows * HEAD_DIM * 4 + N_HEADS * 40 * n * B_BLOCK * 4 + 8 * B_BLOCK * B_BLOCK
    return pl.pallas_call(
        functools.partial(_attn_b_sample_kernel, n_blocks=n_blocks),
        grid=(batch,),
        in_specs=[blk, blk, blk, newest, newest, anywhere, anywhere,
                  const((n, n)), const((n, n)), const((B_BLOCK, B_BLOCK))],
        out_specs=blk,
        out_shape=jax.ShapeDtypeStruct((rows, width), BF16),
        scratch_shapes=[pltpu.VMEM((block_rows, HEAD_DIM), F32), pltpu.VMEM((block_rows, HEAD_DIM), F32),
                        pltpu.SemaphoreType.DMA((2,))],
        compiler_params=pltpu.CompilerParams(
            dimension_semantics=("arbitrary",), vmem_limit_bytes=_vmem_limit(est)),
        name="attn_b_sample",
    )(q, k, v, cache_k, cache_v, cache_k, cache_v,
      _later_matrix(n), _strict_causal_mask(n), _later_matrix(B_BLOCK))


MIX_COLS = 512


def _mix_out_kernel(x_ref, oa_ref, ob_ref, gates_ref, gt_ref, wa_ref, wb_ref, wo_ref, o_ref, y_ref,
                    *, groups, ts):
    d = o_ref.shape[-1]
    oa = oa_ref[...]
    ob = ob_ref[...]
    for c0 in range(0, d, MIX_COLS):
        cols = slice(c0, c0 + MIX_COLS)
        ya = _dot(oa, wa_ref[:, cols])
        yb = _dot(ob, wb_ref[:, cols])
        ga = gates_ref[:, cols].astype(F32)
        gb = gates_ref[:, d + c0:d + c0 + MIX_COLS].astype(F32)
        y_ref[:, cols] = (_sigmoid(ga) * ya + _sigmoid(gb) * yb).astype(BF16)
    y = y_ref[...]
    for c0 in range(0, d, MIX_COLS):
        cols = slice(c0, c0 + MIX_COLS)
        merged = _dot(y, wo_ref[:, cols])
        for gi in range(groups):
            rows = slice(gi * ts, (gi + 1) * ts)
            o_ref[gi, :, cols] = x_ref[gi, :, cols] + gt_ref[gi, :, cols] * merged[rows]


def _mix_out(x, oa, ob, gates, gt, wa, wb, wo, tile_rows):
    batch, seq, d = x.shape
    width = oa.shape[1]
    groups, ts, n_tiles, x_map, b_map = _row_tiling(batch, seq, tile_rows)
    tm = groups * ts
    const = lambda i: (0, 0)
    single = pl.Buffered(1)
    est = (2 * 2 * tm * d * 4 + 2 * 2 * tm * width * 2 + 2 * tm * 2 * d * 2 + tm * d * 2
           + (2 * width * d + d * d) * 2 + 6 * tm * MIX_COLS * 4)
    return pl.pallas_call(
        functools.partial(_mix_out_kernel, groups=groups, ts=ts),
        grid=(n_tiles,),
        in_specs=[
            pl.BlockSpec((groups, ts, d), x_map),
            pl.BlockSpec((tm, width), lambda i: (i, 0)),
            pl.BlockSpec((tm, width), lambda i: (i, 0)),
            pl.BlockSpec((tm, 2 * d), lambda i: (i, 0)),
            pl.BlockSpec((groups, 1, d), b_map),
            pl.BlockSpec((width, d), const, pipeline_mode=single),
            pl.BlockSpec((width, d), const, pipeline_mode=single),
            pl.BlockSpec((d, d), const, pipeline_mode=single),
        ],
        out_specs=pl.BlockSpec((groups, ts, d), x_map),
        out_shape=jax.ShapeDtypeStruct((batch, seq, d), F32),
        scratch_shapes=[pltpu.VMEM((tm, d), BF16)],
        compiler_params=pltpu.CompilerParams(
            dimension_semantics=("parallel",), vmem_limit_bytes=_vmem_limit(est)),
        name="mix_out",
    )(x, oa, ob, gates, gt, wa, wb, wo)


FF_TILE = 512
FF_OUT_COLS = 512


def _ffn_kernel(x_hbm, g_ref, sc_ref, sh_ref, gt_ref, gf_ref, wg_ref, wu_ref, wd_ref, o_ref,
                x_buf, h_ref, x_sem, *, groups, ts, tiles_per_batch):
    i = pl.program_id(0)
    f = pl.program_id(1)

    def x_copy(tile):
        if groups == 1:
            src = x_hbm.at[pl.ds(tile // tiles_per_batch, 1),
                           pl.ds(pl.multiple_of((tile % tiles_per_batch) * ts, ts), ts), :]
        else:
            src = x_hbm.at[pl.ds(pl.multiple_of(tile * groups, groups), groups), :, :]
        return pltpu.make_async_copy(src, x_buf, x_sem.at[0])

    @pl.when(f == 0)
    def _():
        @pl.when(i == 0)
        def _():
            x_copy(0).start()

        x_copy(i).wait()
        _norm_mod_rows(x_buf, g_ref, sc_ref, sh_ref, h_ref, groups, ts, copy_ref=o_ref)

    @pl.when(jnp.logical_and(f == 1, i + 1 < pl.num_programs(0)))
    def _():
        x_copy(i + 1).start()

    h = h_ref[...]
    gate = _dot(h, wg_ref[...])
    up = _dot(h, wu_ref[...])
    hidden = (gate * _sigmoid(gate) * up).astype(BF16)
    d = o_ref.shape[-1]
    for c0 in range(0, d, FF_OUT_COLS):
        cols = slice(c0, c0 + FF_OUT_COLS)
        part = _dot(hidden, wd_ref[:, cols])
        for gi in range(groups):
            o_ref[gi, :, cols] += gt_ref[gi, :, cols] * part[gi * ts:(gi + 1) * ts]

    @pl.when(f == pl.num_programs(1) - 1)
    def _():
        gf = gf_ref[...]
        for gi in range(groups):
            for r0 in range(0, ts, NORM_ROWS):
                rows = slice(r0, r0 + NORM_ROWS)
                x2 = o_ref[gi, rows, :]
                ms = jnp.mean(x2 * x2, axis=-1, keepdims=True)
                o_ref[gi, rows, :] = x2 * lax.rsqrt(ms + EPS) * gf


def _ffn(x, g, sc, sh, gt, g_final, w_gate_up, w_down, tile_rows):
    batch, seq, d = x.shape
    d_ff = w_down.shape[0]
    nf = d_ff // FF_TILE
    groups, ts, n_tiles, x_map, b_map = _row_tiling(batch, seq, tile_rows)
    tm = groups * ts
    xm = lambda i, f: x_map(i)
    bm = lambda i, f: b_map(i)
    vec = pl.BlockSpec((1, d), lambda i, f: (0, 0))
    mod = pl.BlockSpec((groups, 1, d), bm)
    assert nf >= 2
    est = 3 * tm * d * 4 + tm * d * 2 + 2 * 3 * d * FF_TILE * 2 + 5 * tm * FF_TILE * 4
    return pl.pallas_call(
        functools.partial(_ffn_kernel, groups=groups, ts=ts, tiles_per_batch=max(seq // ts, 1)),
        grid=(n_tiles, nf),
        in_specs=[
            pl.BlockSpec(memory_space=pl.ANY), vec, mod, mod, mod, vec,
            pl.BlockSpec((d, FF_TILE), lambda i, f: (0, f)),
            pl.BlockSpec((d, FF_TILE), lambda i, f: (0, nf + f)),
            pl.BlockSpec((FF_TILE, d), lambda i, f: (f, 0)),
        ],
        out_specs=pl.BlockSpec((groups, ts, d), xm),
        out_shape=jax.ShapeDtypeStruct((batch, seq, d), F32),
        scratch_shapes=[pltpu.VMEM((groups, ts, d), F32), pltpu.VMEM((tm, d), BF16),
                        pltpu.SemaphoreType.DMA((1,))],
        compiler_params=pltpu.CompilerParams(
            dimension_semantics=("arbitrary", "arbitrary"), vmem_limit_bytes=_vmem_limit(est)),
        name="ffn",
    )(x, g.reshape(1, d), sc, sh, gt, g_final.reshape(1, d), w_gate_up, w_gate_up, w_down)


def _layer(x, mods, w, attn_a, attn_b, g_final, in_rows, mix_rows, ffn_rows):
    batch, seq, d = x.shape
    sh1, sc1, gt1, sh2, sc2, gt2 = mods
    width = w["w_a_out"].shape[0]
    qa, qb, ka, va, kb, vb, ka4, va4, kb4, vb4, gates = _inproj(
        x, w["g_mix"], sc1, sh1, w["w_in"], in_rows, width)
    oa = attn_a(qa, ka, va)
    ob = attn_b(qb, kb, vb)
    x1 = _mix_out(x, oa, ob, gates, gt1, w["w_a_out"], w["w_b_out"], w["w_o"], mix_rows)
    y = _ffn(x1, w["g_ffn"], sc2, sh2, gt2, g_final, w["w_gate_up"], w["w_down"], ffn_rows)
    heads_of = lambda t: t.reshape(batch, seq, N_HEADS, HEAD_DIM)
    return y, heads_of(ka4), heads_of(va4), heads_of(kb4), heads_of(vb4)


def kernel(x_prompt, x_sample, c_prompt, c_sample, cache_a_k, cache_a_v, cache_b_k, cache_b_v,
           w_ada, b_ada, g_mix, w_in, rel_bias, w_a_out, w_b_out, w_o, g_ffn, w_gate_up, w_down,
           g_final):
    depth = w_in.shape[0]
    assert depth == 1
    batch, seq, d = x_prompt.shape
    dec_batch, dec_seq, _ = x_sample.shape
    past_len = cache_b_k.shape[2]
    l = 0

    n_c = batch + dec_batch
    c_rows = -(-n_c // 8) * 8
    c_all = jnp.concatenate([c_prompt, c_sample, jnp.zeros((c_rows - n_c, d), F32)], axis=0)
    mod_all = _ada(c_all, w_ada[l], b_ada[l])

    def mods_of(rows):
        return tuple(m[:, None, :] for m in jnp.split(rows, 6, axis=-1))

    mods_p = mods_of(mod_all[:batch])
    mods_s = mods_of(mod_all[batch:n_c])

    w = {
        "g_mix": g_mix[l], "g_ffn": g_ffn[l],
        "w_in": w_in[l].astype(BF16), "w_a_out": w_a_out[l].astype(BF16),
        "w_b_out": w_b_out[l].astype(BF16), "w_o": w_o[l].astype(BF16),
        "w_gate_up": w_gate_up[l].astype(BF16), "w_down": w_down[l].astype(BF16),
    }

    yp, ka, va, kb, vb = _layer(
        x_prompt, mods_p, w,
        functools.partial(_attn_a_prompt, rel_bias=rel_bias[l], batch=batch, seq=seq),
        functools.partial(_attn_b_prompt, batch=batch, seq=seq),
        g_final, in_rows=1024, mix_rows=256, ffn_rows=1024)
    keep = min(BAND, seq)
    outs_p = (ka[:, -keep:], va[:, -keep:], kb, vb)

    flat_rows = lambda c: c[l].reshape(dec_batch, c.shape[2] * c.shape[3], c.shape[4])
    cak, cav, cbk, cbv = (flat_rows(c) for c in (cache_a_k, cache_a_v, cache_b_k, cache_b_v))
    ys, ka_s, va_s, kb_s, vb_s = _layer(
        x_sample, mods_s, w,
        functools.partial(_attn_a_sample, cache_k=cak, cache_v=cav, rel_bias=rel_bias[l],
                          batch=dec_batch, n=dec_seq, past_len=past_len),
        functools.partial(_attn_b_sample, cache_k=cbk, cache_v=cbv, batch=dec_batch, n=dec_seq),
        g_final, in_rows=dec_batch * dec_seq, mix_rows=dec_batch * dec_seq,
        ffn_rows=dec_batch * dec_seq)
    outs_s = (ka_s, va_s, kb_s, vb_s)

    return (yp, ys) + tuple(t[None] for t in outs_p) + tuple(t[None] for t in outs_s)
```

```python
import functools

import jax
import jax.numpy as jnp
from jax import lax
from jax.experimental import pallas as pl
from jax.experimental.pallas import tpu as pltpu

F32 = jnp.float32
BF16 = jnp.bfloat16

HEAD_DIM = 128
CHUNK = 64
BAND_CHUNKS = 8
BAND = BAND_CHUNKS * CHUNK
REL_CLIP = 128
EPS = 1e-6
ATTN_SCALE = HEAD_DIM ** -0.5
LOG2_E = 1.4426950408889634
MASK_VALUE = -1e30

V7X_VMEM_BYTES = 64 * 1024 * 1024
V7X_LANES = 128
V7X_BF16_SUBLANES = 16

EXP_ZERO_BELOW = -104.0

NORM_ROWS = 16
NORM_UNROLL = 8


def _norm_unroll(n_rows):
    steps = n_rows // NORM_ROWS
    return NORM_UNROLL if steps % NORM_UNROLL == 0 else 1


def _vmem_limit(estimate_bytes):
    return int(min(V7X_VMEM_BYTES - (4 << 20), max(estimate_bytes * 5 // 4, 16 << 20)))


def _dot(a, b):
    return jnp.dot(a, b, preferred_element_type=F32)


def _dot_nt(a, b):
    return lax.dot_general(a, b, (((1,), (1,)), ((), ())), preferred_element_type=F32)


def _sigmoid(x):
    return 1.0 / (1.0 + jnp.exp(-x))


def _ada_kernel(c_ref, w_ref, b_ref, o_ref):
    c = c_ref[...]
    a = (c * _sigmoid(c)).astype(BF16)
    o_ref[...] = _dot(a, w_ref[...].astype(BF16)) + b_ref[...]


def _ada(c_all, w_ada, b_ada):
    rows, d = c_all.shape
    n = w_ada.shape[1]
    tn = 1024
    est = 2 * d * tn * 4 + d * tn * 2 + 4 * rows * (d + tn) * 4
    return pl.pallas_call(
        _ada_kernel,
        grid=(n // tn,),
        in_specs=[
            pl.BlockSpec((rows, d), lambda j: (0, 0)),
            pl.BlockSpec((d, tn), lambda j: (0, j)),
            pl.BlockSpec((1, tn), lambda j: (0, j)),
        ],
        out_specs=pl.BlockSpec((rows, tn), lambda j: (0, j)),
        out_shape=jax.ShapeDtypeStruct((rows, n), F32),
        compiler_params=pltpu.CompilerParams(
            dimension_semantics=("parallel",), vmem_limit_bytes=_vmem_limit(est)),
        name="ada",
    )(c_all, w_ada, b_ada.reshape(1, n))


def _norm_mod_rows(x_ref, g_ref, sc_ref, sh_ref, h_ref, groups, ts, copy_ref=None):
    gvec = g_ref[...]
    for gi in range(groups):
        scale = gvec * (1.0 + sc_ref[gi])
        shift = sh_ref[gi]

        def body(r, carry, gi=gi, scale=scale, shift=shift):
            r0 = pl.multiple_of(r * NORM_ROWS, NORM_ROWS)
            x = x_ref[gi, pl.ds(r0, NORM_ROWS), :]
            if copy_ref is not None:
                copy_ref[gi, pl.ds(r0, NORM_ROWS), :] = x
            ms = jnp.mean(x * x, axis=-1, keepdims=True)
            h = x * lax.rsqrt(ms + EPS) * scale + shift
            h_ref[pl.ds(pl.multiple_of(gi * ts + r0, NORM_ROWS), NORM_ROWS), :] = h.astype(BF16)
            return carry

        lax.fori_loop(0, ts // NORM_ROWS, body, 0, unroll=_norm_unroll(ts))


def _row_tiling(batch, seq, tile_rows):
    if seq >= tile_rows:
        assert seq % tile_rows == 0
        groups, ts, ns = 1, tile_rows, seq // tile_rows
    else:
        assert tile_rows % seq == 0 and batch % (tile_rows // seq) == 0
        groups, ts, ns = tile_rows // seq, seq, 1
    n_tiles = batch * seq // (groups * ts)
    if groups == 1:
        x_map = lambda i: (i // ns, i % ns, 0)
        b_map = lambda i: (i // ns, 0, 0)
    else:
        x_map = lambda i: (i, 0, 0)
        b_map = lambda i: (i, 0, 0)
    return groups, ts, n_tiles, x_map, b_map


N_QKV = 6
N_HEADS = 8


N_STAGE = 2


def _inproj_kernel(x_ref, g_ref, sc_ref, sh_ref, w_in_ref,
                   qa_hbm, qb_hbm, ka_hbm, va_hbm, kb_hbm, vb_hbm,
                   ka4_hbm, va4_hbm, kb4_hbm, vb4_hbm, gates_hbm, *rest, groups, ts, cast_w):
    if cast_w:
        w_ref, h_ref, stage16, stage32, sem16, sem32 = rest
        w_ref[...] = w_in_ref[...].astype(BF16)
    else:
        w_ref = w_in_ref
        h_ref, stage16, stage32, sem16, sem32 = rest
    i = pl.program_id(0)
    j = pl.program_id(1)
    tm = groups * ts
    width = w_ref.shape[1]
    rows = pl.ds(pl.multiple_of(i * tm, tm), tm)
    rows4 = pl.ds(pl.multiple_of(i * (tm * N_HEADS), tm * N_HEADS), tm * N_HEADS)
    slot16 = j % N_STAGE
    kv_slot = {1: 0, 2: 1, 4: 0, 5: 1}

    def copy16(slot, dst):
        return pltpu.make_async_copy(stage16.at[slot], dst, sem16.at[slot])

    def copy32(slot, dst_hbm):
        return pltpu.make_async_copy(stage32.at[slot], dst_hbm.at[rows4, :], sem32.at[slot])

    def free16():
        @pl.when(jnp.logical_or(i > 0, j >= N_STAGE))
        def _():
            copy16(slot16, qa_hbm.at[rows, :]).wait()

    @pl.when(j == 0)
    def _():
        _norm_mod_rows(x_ref, g_ref, sc_ref, sh_ref, h_ref, groups, ts)

    for idx, q_hbm, scale in ((0, qa_hbm, ATTN_SCALE * LOG2_E), (3, qb_hbm, ATTN_SCALE)):
        @pl.when(j == idx)
        def _(q_hbm=q_hbm, scale=scale):
            free16()
            stage16[slot16] = (_dot(h_ref[...], w_ref[...]) * scale).astype(BF16)
            copy16(slot16, q_hbm.at[rows, :]).start()

    for idx, kv_hbm, kv4_hbm in ((1, ka_hbm, ka4_hbm), (2, va_hbm, va4_hbm),
                                 (4, kb_hbm, kb4_hbm), (5, vb_hbm, vb4_hbm)):
        @pl.when(j == idx)
        def _(idx=idx, kv_hbm=kv_hbm, kv4_hbm=kv4_hbm):
            slot32 = kv_slot[idx]
            free16()
            if idx in (4, 5):
                copy32(slot32, kv4_hbm).wait()
            else:
                @pl.when(i > 0)
                def _():
                    copy32(slot32, kv4_hbm).wait()
            p = _dot(h_ref[...], w_ref[...])
            stage16[slot16] = p.astype(BF16)
            copy16(slot16, kv_hbm.at[rows, :]).start()
            for hd in range(N_HEADS):
                stage32[slot32, pl.ds(hd, tm, stride=N_HEADS), :] = p[:, hd * HEAD_DIM:(hd + 1) * HEAD_DIM]
            copy32(slot32, kv4_hbm).start()

    @pl.when(j >= N_QKV)
    def _():
        free16()
        stage16[slot16] = _dot(h_ref[...], w_ref[...]).astype(BF16)
        col0 = pl.multiple_of((j - N_QKV) * width, width)
        copy16(slot16, gates_hbm.at[rows, pl.ds(col0, width)]).start()

        @pl.when(jnp.logical_and(i == pl.num_programs(0) - 1, j == pl.num_programs(1) - 1))
        def _():
            for slot in range(N_STAGE):
                copy16(slot, qa_hbm.at[rows, :]).wait()
                copy32(slot, ka4_hbm).wait()


def _inproj(x, g, sc, sh, w_in, tile_rows, width, cast_w=False):
    batch, seq, d = x.shape
    rows = batch * seq
    n_cols = w_in.shape[1]
    assert width == N_HEADS * HEAD_DIM and n_cols % width == 0
    nj = n_cols // width
    n_gate_tiles = nj - N_QKV
    assert nj % N_STAGE == 0 and n_gate_tiles >= N_STAGE
    groups, ts, n_tiles, x_map, b_map = _row_tiling(batch, seq, tile_rows)
    tm = groups * ts
    xm = lambda i, j: x_map(i)
    bm = lambda i, j: b_map(i)
    anywhere = pl.BlockSpec(memory_space=pl.ANY)
    w_spec = pl.BlockSpec((d, width), lambda i, j: (0, j))
    est = (2 * tm * d * 4 + 2 * d * width * w_in.dtype.itemsize + tm * d * 2
           + N_STAGE * tm * width * (2 + 4) + 2 * tm * width * 4)
    bf16_rows = jax.ShapeDtypeStruct((rows, width), BF16)
    f32_flat = jax.ShapeDtypeStruct((rows * N_HEADS, HEAD_DIM), F32)
    cast_specs, cast_shapes = [], []
    if cast_w:
        assert n_tiles == 1 and w_in.dtype == F32
        cast_specs, cast_shapes = [w_spec], [jax.ShapeDtypeStruct(w_in.shape, BF16)]
        est += 2 * d * width * 2
    return pl.pallas_call(
        functools.partial(_inproj_kernel, groups=groups, ts=ts, cast_w=cast_w),
        grid=(n_tiles, nj),
        in_specs=[
            pl.BlockSpec((groups, ts, d), xm),
            pl.BlockSpec((1, d), lambda i, j: (0, 0)),
            pl.BlockSpec((groups, 1, d), bm),
            pl.BlockSpec((groups, 1, d), bm),
            w_spec,
        ],
        out_specs=[anywhere] * 11 + cast_specs,
        out_shape=[bf16_rows] * 6 + [f32_flat] * 4
        + [jax.ShapeDtypeStruct((rows, n_gate_tiles * width), BF16)] + cast_shapes,
        scratch_shapes=[pltpu.VMEM((tm, d), BF16),
                        pltpu.VMEM((N_STAGE, tm, width), BF16),
                        pltpu.VMEM((N_STAGE, tm * N_HEADS, HEAD_DIM), F32),
                        pltpu.SemaphoreType.DMA((N_STAGE,)), pltpu.SemaphoreType.DMA((N_STAGE,))],
        compiler_params=pltpu.CompilerParams(
            dimension_semantics=("arbitrary", "arbitrary"), vmem_limit_bytes=_vmem_limit(est)),
        name="inproj",
    )(x, g.reshape(1, d), sc, sh, w_in)


A_BLOCK = 256


A_HEADS = 8
A_VARIANTS = 3
TOEPLITZ_ROW = 1024


def _toeplitz_row(rel_bias, offset, n_cols):
    m = jnp.arange(TOEPLITZ_ROW)
    m = jnp.where(m < n_cols, m, m - TOEPLITZ_ROW)
    idx = jnp.clip(offset - m, -REL_CLIP, REL_CLIP) + REL_CLIP
    return rel_bias.astype(F32)[:, None, idx]


def _toeplitz_bias(row, n_rows):
    return pltpu.roll(jnp.broadcast_to(row, (n_rows, TOEPLITZ_ROW)), 0, 1, stride=1, stride_axis=0)


def _attn_a_prompt_kernel(g_ref, q_ref, k0_ref, k1_ref, k2_ref, v0_ref, v1_ref, v2_ref, o_ref, bias_ref):
    qi = pl.program_id(2)
    n_keys = 3 * A_BLOCK

    @pl.when(qi == 0)
    def _():
        qc = lax.broadcasted_iota(jnp.int32, (A_BLOCK, n_keys), 0) // CHUNK
        col = lax.broadcasted_iota(jnp.int32, (A_BLOCK, n_keys), 1)
        kc = col // CHUNK - BAND // CHUNK
        in_band = jnp.logical_and(kc <= qc, kc >= qc - BAND_CHUNKS)
        for hh in range(A_HEADS):
            rel = _toeplitz_bias(g_ref[hh], A_BLOCK)[:, :n_keys] * LOG2_E
            full = jnp.where(in_band, rel, MASK_VALUE)
            for v in range(A_VARIANTS):
                bias_ref[v, hh] = jnp.where(col >= (A_VARIANTS - 1 - v) * A_BLOCK, full, MASK_VALUE)

    variant = jnp.minimum(qi, A_VARIANTS - 1)
    for hh in range(A_HEADS):
        cols = slice(hh * HEAD_DIM, (hh + 1) * HEAD_DIM)
        q = q_ref[:, cols]
        scores = [
            _dot_nt(q, k_ref[:, cols].astype(BF16))
            + bias_ref[variant, hh, :, jb * A_BLOCK:(jb + 1) * A_BLOCK]
            for jb, k_ref in enumerate((k0_ref, k1_ref, k2_ref))]
        m = jnp.max(functools.reduce(jnp.maximum, scores), axis=-1, keepdims=True)
        es = [jnp.exp2(s - m) for s in scores]
        l = jnp.sum(functools.reduce(jnp.add, es), axis=-1, keepdims=True)
        acc = functools.reduce(jnp.add, [
            _dot(e.astype(BF16), v_ref[:, cols].astype(BF16))
            for e, v_ref in zip(es, (v0_ref, v1_ref, v2_ref))])
        o_ref[:, cols] = (acc / l).astype(o_ref.dtype)


def _attn_a_prompt(q, k, v, rel_bias, batch, seq):
    rows, width = q.shape
    heads = width // HEAD_DIM
    nq = seq // A_BLOCK
    assert BAND == 2 * A_BLOCK and seq % A_BLOCK == 0 and heads % A_HEADS == 0
    assert 4 * A_BLOCK - 1 <= TOEPLITZ_ROW
    g = _toeplitz_row(rel_bias, BAND, 3 * A_BLOCK)
    blk = (A_BLOCK, A_HEADS * HEAD_DIM)

    def kv_spec(jb):
        return pl.BlockSpec(blk, lambda b, h, i: (b * nq + jnp.maximum(i + (jb - 2), 0), h))

    est = (2 * A_BLOCK * A_HEADS * HEAD_DIM * (2 + 6 * 4 + 2)
           + A_HEADS * (A_VARIANTS * A_BLOCK * 3 * A_BLOCK * 4 + 8 * A_BLOCK * 3 * A_BLOCK * 4))
    return pl.pallas_call(
        _attn_a_prompt_kernel,
        grid=(batch, heads // A_HEADS, nq),
        in_specs=[pl.BlockSpec((A_HEADS, 1, TOEPLITZ_ROW), lambda b, h, i: (h, 0, 0)),
                  pl.BlockSpec(blk, lambda b, h, i: (b * nq + i, h)),
                  kv_spec(0), kv_spec(1), kv_spec(2), kv_spec(0), kv_spec(1), kv_spec(2)],
        out_specs=pl.BlockSpec(blk, lambda b, h, i: (b * nq + i, h)),
        out_shape=jax.ShapeDtypeStruct((rows, width), BF16),
        scratch_shapes=[pltpu.VMEM((A_VARIANTS, A_HEADS, A_BLOCK, 3 * A_BLOCK), F32)],
        compiler_params=pltpu.CompilerParams(
            dimension_semantics=("parallel", "parallel", "arbitrary"),
            vmem_limit_bytes=_vmem_limit(est)),
        name="attn_a_prompt",
    )(g, q, k, k, k, v, v, v)


def _head_rows(ref, first_pos, n_pos, head):
    return ref[pl.ds(first_pos * N_HEADS + head, n_pos, stride=N_HEADS), :]


def _attn_a_sample_kernel(g_ref, q_ref, kc_ref, vc_ref, kn_ref, vn_ref, o_ref, *, lc):
    n = q_ref.shape[0]
    kc_ref = kc_ref.at[0]
    vc_ref = vc_ref.at[0]
    for hh in range(N_HEADS):
        cols = slice(hh * HEAD_DIM, (hh + 1) * HEAD_DIM)
        q = q_ref[:, cols]
        bias = _toeplitz_bias(g_ref[hh], n) * LOG2_E
        sc = _dot_nt(q, _head_rows(kc_ref, 0, lc, hh).astype(BF16)) + bias[:, :lc]
        sn = _dot_nt(q, kn_ref[:, cols]) + bias[:, lc:lc + n]
        m = jnp.maximum(jnp.max(sc, axis=-1, keepdims=True), jnp.max(sn, axis=-1, keepdims=True))
        ec = jnp.exp2(sc - m)
        en = jnp.exp2(sn - m)
        l = jnp.sum(ec, axis=-1, keepdims=True) + jnp.sum(en, axis=-1, keepdims=True)
        acc = (_dot(ec.astype(BF16), _head_rows(vc_ref, 0, lc, hh).astype(BF16))
               + _dot(en.astype(BF16), vn_ref[:, cols]))
        o_ref[:, cols] = (acc / l).astype(o_ref.dtype)


def _attn_a_sample(q, k, v, cache_k, cache_v, rel_bias, batch, n, past_len):
    rows, width = q.shape
    lc = cache_k.shape[1] // N_HEADS
    assert past_len >= lc and lc + 2 * n - 1 <= TOEPLITZ_ROW and lc % V7X_LANES == 0
    assert width == N_HEADS * HEAD_DIM
    g = _toeplitz_row(rel_bias, lc, lc + n)
    blk = pl.BlockSpec((n, width), lambda b: (b, 0))
    cblk = pl.BlockSpec((1, lc * N_HEADS, HEAD_DIM), lambda b: (b, 0, 0))
    est = 2 * 2 * lc * width * 4 + 16 * n * lc * 4 * N_HEADS
    return pl.pallas_call(
        functools.partial(_attn_a_sample_kernel, lc=lc),
        grid=(batch,),
        in_specs=[pl.BlockSpec((N_HEADS, 1, TOEPLITZ_ROW), lambda b: (0, 0, 0)),
                  blk, cblk, cblk, blk, blk],
        out_specs=blk,
        out_shape=jax.ShapeDtypeStruct((rows, width), BF16),
        compiler_params=pltpu.CompilerParams(
            dimension_semantics=("parallel",), vmem_limit_bytes=_vmem_limit(est)),
        name="attn_a_sample",
    )(g, q, cache_k, cache_v, k, v)


B_BLOCK = 256
B_HEADS = 4


def _later_matrix(n):
    j = jnp.arange(n)[:, None]
    s = jnp.arange(n)[None, :]
    return (j > s).astype(BF16)


def _strict_causal_mask(n):
    t = jnp.arange(n)[:, None]
    s = jnp.arange(n)[None, :]
    return jnp.where(s < t, 0.0, MASK_VALUE).astype(F32)


def _stick_terms(q, k, later_mat, z_mask=None, valid=None):
    z = _dot_nt(q, k)
    if z_mask is not None:
        z = z + z_mask
    if valid is not None:
        z = jnp.where(valid, z, MASK_VALUE)
    log_1m = -(jnp.maximum(z, 0.0) + jnp.log(1.0 + jnp.exp(-jnp.abs(z))))
    hi = log_1m.astype(BF16)
    lo = (log_1m - hi.astype(F32)).astype(BF16)
    later = _dot(hi, later_mat) + _dot(lo, later_mat)
    return z + log_1m + later, jnp.sum(log_1m, axis=-1, keepdims=True)


def _stick_sweep_past(qs, load_block, n_blocks, cs, accs, later_mat):
    heads = len(qs)

    def cond(carry):
        j, cs, _ = carry
        return jnp.logical_and(j >= 0, jnp.max(functools.reduce(jnp.maximum, cs)) > EXP_ZERO_BELOW)

    def body(carry):
        j, cs, accs = carry
        k_of, v_of = load_block(j)
        new_cs, new_accs = [], []
        for hh in range(heads):
            logit, row_sum = _stick_terms(qs[hh], k_of(hh).astype(BF16), later_mat)
            weights = jnp.exp(logit + cs[hh]).astype(BF16)
            new_accs.append(accs[hh] + _dot(weights, v_of(hh).astype(BF16)))
            new_cs.append(cs[hh] + row_sum)
        return j - 1, tuple(new_cs), tuple(new_accs)

    _, _, accs = lax.while_loop(cond, body, (n_blocks - 1, tuple(cs), tuple(accs)))
    return accs


def _stick_heads(q_ref, own, prev, later_own, mask_own, later_mat, prev_valid, load_block, n_blocks,
                 o_ref):
    heads = q_ref.shape[1] // HEAD_DIM
    qs, cs, accs = [], [], []
    for hh in range(heads):
        q = q_ref[:, hh * HEAD_DIM:(hh + 1) * HEAD_DIM]
        logit0, sum0 = _stick_terms(q, own[0](hh).astype(BF16), later_own, z_mask=mask_own)
        logit1, sum1 = _stick_terms(q, prev[0](hh).astype(BF16), later_mat, valid=prev_valid)
        accs.append(_dot(jnp.exp(logit0).astype(BF16), own[1](hh).astype(BF16))
                    + _dot(jnp.exp(logit1 + sum0).astype(BF16), prev[1](hh).astype(BF16)))
        cs.append(sum0 + sum1)
        qs.append(q)
    accs = _stick_sweep_past(qs, load_block, n_blocks, cs, accs, later_mat)
    for hh in range(heads):
        o_ref[:, hh * HEAD_DIM:(hh + 1) * HEAD_DIM] = accs[hh].astype(o_ref.dtype)


def _lane_heads(k_ref, v_ref, rows):
    cols = lambda hh: slice(hh * HEAD_DIM, (hh + 1) * HEAD_DIM)
    return (lambda hh: k_ref[rows, cols(hh)]), (lambda hh: v_ref[rows, cols(hh)])


def _row_heads(k_ref, v_ref, n_pos):
    return (lambda hh: _head_rows(k_ref, 0, n_pos, hh)), (lambda hh: _head_rows(v_ref, 0, n_pos, hh))


def _attn_b_prompt_kernel(q_ref, k_ref, v_ref, u_ref, mask_ref, o_ref):
    qi = pl.program_id(2)
    k_ref = k_ref.at[0]
    v_ref = v_ref.at[0]
    later_mat = u_ref[...]
    block = lambda j: _lane_heads(k_ref, v_ref, pl.ds(pl.multiple_of(j * B_BLOCK, B_BLOCK), B_BLOCK))
    _stick_heads(q_ref, block(qi), block(jnp.maximum(qi - 1, 0)), later_mat, mask_ref[...], later_mat,
                 qi > 0, block, qi - 1, o_ref)


def _attn_b_prompt(q, k, v, batch, seq):
    rows, width = q.shape
    heads = width // HEAD_DIM
    assert heads % B_HEADS == 0
    nq = seq // B_BLOCK
    k3 = k.reshape(batch, seq, width)
    v3 = v.reshape(batch, seq, width)
    lanes = B_HEADS * HEAD_DIM
    blk = (B_BLOCK, lanes)
    kv_spec = pl.BlockSpec((1, seq, lanes), lambda b, h, i: (b, 0, h))
    const = pl.BlockSpec((B_BLOCK, B_BLOCK), lambda b, h, i: (0, 0))
    est = 2 * 2 * seq * lanes * k.dtype.itemsize + B_HEADS * 40 * B_BLOCK * B_BLOCK * 4
    return pl.pallas_call(
        _attn_b_prompt_kernel,
        grid=(batch, heads // B_HEADS, nq),
        in_specs=[pl.BlockSpec(blk, lambda b, h, i: (b * nq + i, h)), kv_spec, kv_spec, const, const],
        out_specs=pl.BlockSpec(blk, lambda b, h, i: (b * nq + i, h)),
        out_shape=jax.ShapeDtypeStruct((rows, width), BF16),
        compiler_params=pltpu.CompilerParams(
            dimension_semantics=("parallel", "parallel", "parallel"),
            vmem_limit_bytes=_vmem_limit(est)),
        name="attn_b_prompt",
    )(q, k3, v3, _later_matrix(B_BLOCK), _strict_causal_mask(B_BLOCK))


def _attn_b_sample_kernel(q_ref, kn_ref, vn_ref, k1_ref, v1_ref, kc_hbm, vc_hbm, un_ref, mask_ref, u_ref,
                          o_ref, kbuf, vbuf, sems, *, n_blocks):
    b = pl.program_id(0)
    block_rows = B_BLOCK * N_HEADS

    def fetch_block(j):
        rows = pl.ds(pl.multiple_of(j * block_rows, block_rows), block_rows)
        copies = (pltpu.make_async_copy(kc_hbm.at[b, rows, :], kbuf, sems.at[0]),
                  pltpu.make_async_copy(vc_hbm.at[b, rows, :], vbuf, sems.at[1]))
        for cp in copies:
            cp.start()
        for cp in copies:
            cp.wait()
        return _row_heads(kbuf, vbuf, B_BLOCK)

    _stick_heads(q_ref, _lane_heads(kn_ref, vn_ref, slice(None)),
                 _row_heads(k1_ref.at[0], v1_ref.at[0], B_BLOCK),
                 un_ref[...], mask_ref[...], u_ref[...], None, fetch_block, n_blocks - 1, o_ref)


def _attn_b_sample(q, k, v, cache_k, cache_v, batch, n):
    rows, width = q.shape
    lc = cache_k.shape[1] // N_HEADS
    assert lc % B_BLOCK == 0 and lc >= B_BLOCK and width == N_HEADS * HEAD_DIM
    n_blocks = lc // B_BLOCK
    block_rows = B_BLOCK * N_HEADS
    blk = pl.BlockSpec((n, width), lambda b: (b, 0))
    newest = pl.BlockSpec((1, block_rows, HEAD_DIM), lambda b: (b, n_blocks - 1, 0))
    anywhere = pl.BlockSpec(memory_space=pl.ANY)
    const = lambda shape: pl.BlockSpec(shape, lambda b: (0, 0))
    est = 6 * block_rows * HEAD_DIM * 4 + N_HEADS * 40 * n * B_BLOCK * 4 + 8 * B_BLOCK * B_BLOCK
    return pl.pallas_call(
        functools.partial(_attn_b_sample_kernel, n_blocks=n_blocks),
        grid=(batch,),
        in_specs=[blk, blk, blk, newest, newest, anywhere, anywhere,
                  const((n, n)), const((n, n)), const((B_BLOCK, B_BLOCK))],
        out_specs=blk,
        out_shape=jax.ShapeDtypeStruct((rows, width), BF16),
        scratch_shapes=[pltpu.VMEM((block_rows, HEAD_DIM), F32), pltpu.VMEM((block_rows, HEAD_DIM), F32),
                        pltpu.SemaphoreType.DMA((2,))],
        compiler_params=pltpu.CompilerParams(
            dimension_semantics=("arbitrary",), vmem_limit_bytes=_vmem_limit(est)),
        name="attn_b_sample",
    )(q, k, v, cache_k, cache_v, cache_k, cache_v,
      _later_matrix(n), _strict_causal_mask(n), _later_matrix(B_BLOCK))


MIX_COLS = 512


def _mix_out_kernel(x_ref, oa_ref, ob_ref, gates_ref, gt_ref, wa_ref, wb_ref, wo_ref, o_ref, y_ref,
                    *, groups, ts):
    d = o_ref.shape[-1]
    oa = oa_ref[...]
    ob = ob_ref[...]
    for c0 in range(0, d, MIX_COLS):
        cols = slice(c0, c0 + MIX_COLS)
        ya = _dot(oa, wa_ref[:, cols])
        yb = _dot(ob, wb_ref[:, cols])
        ga = gates_ref[:, cols].astype(F32)
        gb = gates_ref[:, d + c0:d + c0 + MIX_COLS].astype(F32)
        y_ref[:, cols] = (_sigmoid(ga) * ya + _sigmoid(gb) * yb).astype(BF16)
    y = y_ref[...]
    for c0 in range(0, d, MIX_COLS):
        cols = slice(c0, c0 + MIX_COLS)
        merged = _dot(y, wo_ref[:, cols])
        for gi in range(groups):
            rows = slice(gi * ts, (gi + 1) * ts)
            o_ref[gi, :, cols] = x_ref[gi, :, cols] + gt_ref[gi, :, cols] * merged[rows]


def _mix_out(x, oa, ob, gates, gt, wa, wb, wo, tile_rows):
    batch, seq, d = x.shape
    width = oa.shape[1]
    groups, ts, n_tiles, x_map, b_map = _row_tiling(batch, seq, tile_rows)
    tm = groups * ts
    const = lambda i: (0, 0)
    single = pl.Buffered(1)
    est = (2 * 2 * tm * d * 4 + 2 * 2 * tm * width * 2 + 2 * tm * 2 * d * 2 + tm * d * 2
           + (2 * width * d + d * d) * 2 + 6 * tm * MIX_COLS * 4)
    return pl.pallas_call(
        functools.partial(_mix_out_kernel, groups=groups, ts=ts),
        grid=(n_tiles,),
        in_specs=[
            pl.BlockSpec((groups, ts, d), x_map),
            pl.BlockSpec((tm, width), lambda i: (i, 0)),
            pl.BlockSpec((tm, width), lambda i: (i, 0)),
            pl.BlockSpec((tm, 2 * d), lambda i: (i, 0)),
            pl.BlockSpec((groups, 1, d), b_map),
            pl.BlockSpec((width, d), const, pipeline_mode=single),
            pl.BlockSpec((width, d), const, pipeline_mode=single),
            pl.BlockSpec((d, d), const, pipeline_mode=single),
        ],
        out_specs=pl.BlockSpec((groups, ts, d), x_map),
        out_shape=jax.ShapeDtypeStruct((batch, seq, d), F32),
        scratch_shapes=[pltpu.VMEM((tm, d), BF16)],
        compiler_params=pltpu.CompilerParams(
            dimension_semantics=("parallel",), vmem_limit_bytes=_vmem_limit(est)),
        name="mix_out",
    )(x, oa, ob, gates, gt, wa, wb, wo)


FF_TILE = 512
FF_OUT_COLS = 512


def _ffn_kernel(x_hbm, g_ref, sc_ref, sh_ref, gt_ref, gf_ref, wg_in_ref, wu_in_ref, wd_in_ref, o_ref,
                *rest, groups, ts, tiles_per_batch, cast_w):
    if cast_w:
        wg_ref, wu_ref, wd_ref, x_buf, h_ref, x_sem = rest
        wg_ref[...] = wg_in_ref[...].astype(BF16)
        wu_ref[...] = wu_in_ref[...].astype(BF16)
        wd_ref[...] = wd_in_ref[...].astype(BF16)
    else:
        wg_ref, wu_ref, wd_ref = wg_in_ref, wu_in_ref, wd_in_ref
        x_buf, h_ref, x_sem = rest
    i = pl.program_id(0)
    f = pl.program_id(1)

    def x_copy(tile):
        if groups == 1:
            src = x_hbm.at[pl.ds(tile // tiles_per_batch, 1),
                           pl.ds(pl.multiple_of((tile % tiles_per_batch) * ts, ts), ts), :]
        else:
            src = x_hbm.at[pl.ds(pl.multiple_of(tile * groups, groups), groups), :, :]
        return pltpu.make_async_copy(src, x_buf, x_sem.at[0])

    @pl.when(f == 0)
    def _():
        @pl.when(i == 0)
        def _():
            x_copy(0).start()

        x_copy(i).wait()
        _norm_mod_rows(x_buf, g_ref, sc_ref, sh_ref, h_ref, groups, ts, copy_ref=o_ref)

    @pl.when(jnp.logical_and(f == 1, i + 1 < pl.num_programs(0)))
    def _():
        x_copy(i + 1).start()

    h = h_ref[...]
    gate = _dot(h, wg_ref[...])
    up = _dot(h, wu_ref[...])
    hidden = (gate * _sigmoid(gate) * up).astype(BF16)
    d = o_ref.shape[-1]
    for c0 in range(0, d, FF_OUT_COLS):
        cols = slice(c0, c0 + FF_OUT_COLS)
        part = _dot(hidden, wd_ref[:, cols])
        for gi in range(groups):
            o_ref[gi, :, cols] += gt_ref[gi, :, cols] * part[gi * ts:(gi + 1) * ts]

    @pl.when(f == pl.num_programs(1) - 1)
    def _():
        gf = gf_ref[...]
        for gi in range(groups):
            for r0 in range(0, ts, NORM_ROWS):
                rows = slice(r0, r0 + NORM_ROWS)
                x2 = o_ref[gi, rows, :]
                ms = jnp.mean(x2 * x2, axis=-1, keepdims=True)
                o_ref[gi, rows, :] = x2 * lax.rsqrt(ms + EPS) * gf


def _ffn(x, g, sc, sh, gt, g_final, w_gate, w_up, w_down, tile_rows, cast_w=False):
    batch, seq, d = x.shape
    d_ff = w_down.shape[0]
    nf = d_ff // FF_TILE
    up_offset = nf if w_up.shape[1] == 2 * d_ff else 0
    groups, ts, n_tiles, x_map, b_map = _row_tiling(batch, seq, tile_rows)
    tm = groups * ts
    xm = lambda i, f: x_map(i)
    bm = lambda i, f: b_map(i)
    vec = pl.BlockSpec((1, d), lambda i, f: (0, 0))
    mod = pl.BlockSpec((groups, 1, d), bm)
    assert nf >= 2
    est = (3 * tm * d * 4 + tm * d * 2 + 2 * 3 * d * FF_TILE * w_down.dtype.itemsize
           + 5 * tm * FF_TILE * 4)
    col_tile = pl.BlockSpec((d, FF_TILE), lambda i, f: (0, f))
    row_tile = pl.BlockSpec((FF_TILE, d), lambda i, f: (f, 0))
    cast_specs, cast_shapes = [], []
    if cast_w:
        assert n_tiles == 1 and w_down.dtype == F32 and up_offset == nf
        cast_specs = [col_tile, col_tile, row_tile]
        cast_shapes = [jax.ShapeDtypeStruct((d, d_ff), BF16)] * 2 + [jax.ShapeDtypeStruct((d_ff, d), BF16)]
        est += 2 * 3 * d * FF_TILE * 2
    return pl.pallas_call(
        functools.partial(_ffn_kernel, groups=groups, ts=ts, tiles_per_batch=max(seq // ts, 1),
                          cast_w=cast_w),
        grid=(n_tiles, nf),
        in_specs=[
            pl.BlockSpec(memory_space=pl.ANY), vec, mod, mod, mod, vec,
            col_tile,
            pl.BlockSpec((d, FF_TILE), lambda i, f: (0, up_offset + f)),
            row_tile,
        ],
        out_specs=[pl.BlockSpec((groups, ts, d), xm)] + cast_specs,
        out_shape=[jax.ShapeDtypeStruct((batch, seq, d), F32)] + cast_shapes,
        scratch_shapes=[pltpu.VMEM((groups, ts, d), F32), pltpu.VMEM((tm, d), BF16),
                        pltpu.SemaphoreType.DMA((1,))],
        compiler_params=pltpu.CompilerParams(
            dimension_semantics=("arbitrary", "arbitrary"), vmem_limit_bytes=_vmem_limit(est)),
        name="ffn",
    )(x, g.reshape(1, d), sc, sh, gt, g_final.reshape(1, d), w_gate, w_up, w_down)


def _layer(x, mods, w, attn_a, attn_b, g_final, in_rows, mix_rows, ffn_rows, cast_w=False):
    batch, seq, d = x.shape
    sh1, sc1, gt1, sh2, sc2, gt2 = mods
    width = w["w_a_out"].shape[0]
    qa, qb, ka, va, kb, vb, ka4, va4, kb4, vb4, gates, *w_in16 = _inproj(
        x, w["g_mix"], sc1, sh1, w["w_in"], in_rows, width, cast_w)
    oa = attn_a(qa, ka, va)
    ob = attn_b(qb, kb, vb)
    x1 = _mix_out(x, oa, ob, gates, gt1, w["w_a_out"], w["w_b_out"], w["w_o"], mix_rows)
    y, *w_ffn16 = _ffn(x1, w["g_ffn"], sc2, sh2, gt2, g_final, w["w_gate"], w["w_up"], w["w_down"],
                       ffn_rows, cast_w)
    heads_of = lambda t: t.reshape(batch, seq, N_HEADS, HEAD_DIM)
    cast = dict(zip(("w_in", "w_gate", "w_up", "w_down"), w_in16 + w_ffn16))
    return (y, heads_of(ka4), heads_of(va4), heads_of(kb4), heads_of(vb4)), cast


def kernel(x_prompt, x_sample, c_prompt, c_sample, cache_a_k, cache_a_v, cache_b_k, cache_b_v,
           w_ada, b_ada, g_mix, w_in, rel_bias, w_a_out, w_b_out, w_o, g_ffn, w_gate_up, w_down,
           g_final):
    depth = w_in.shape[0]
    assert depth == 1
    batch, seq, d = x_prompt.shape
    dec_batch, dec_seq, _ = x_sample.shape
    past_len = cache_b_k.shape[2]
    l = 0

    n_c = batch + dec_batch
    c_rows = -(-n_c // 8) * 8
    c_all = jnp.concatenate([c_prompt, c_sample, jnp.zeros((c_rows - n_c, d), F32)], axis=0)
    mod_all = _ada(c_all, w_ada[l], b_ada[l])

    def mods_of(rows):
        return tuple(m[:, None, :] for m in jnp.split(rows, 6, axis=-1))

    mods_p = mods_of(mod_all[:batch])
    mods_s = mods_of(mod_all[batch:n_c])

    w_sample = {
        "g_mix": g_mix[l], "g_ffn": g_ffn[l],
        "w_a_out": w_a_out[l].astype(BF16), "w_b_out": w_b_out[l].astype(BF16),
        "w_o": w_o[l].astype(BF16),
        "w_in": w_in[l], "w_gate": w_gate_up[l], "w_up": w_gate_up[l], "w_down": w_down[l],
    }

    flat_rows = lambda c: c[l].reshape(dec_batch, c.shape[2] * c.shape[3], c.shape[4])
    cak, cav, cbk, cbv = (flat_rows(c) for c in (cache_a_k, cache_a_v, cache_b_k, cache_b_v))
    (ys, *outs_s), cast = _layer(
        x_sample, mods_s, w_sample,
        functools.partial(_attn_a_sample, cache_k=cak, cache_v=cav, rel_bias=rel_bias[l],
                          batch=dec_batch, n=dec_seq, past_len=past_len),
        functools.partial(_attn_b_sample, cache_k=cbk, cache_v=cbv, batch=dec_batch, n=dec_seq),
        g_final, in_rows=dec_batch * dec_seq, mix_rows=dec_batch * dec_seq,
        ffn_rows=dec_batch * dec_seq, cast_w=True)

    (yp, ka, va, kb, vb), _ = _layer(
        x_prompt, mods_p, {**w_sample, **cast},
        functools.partial(_attn_a_prompt, rel_bias=rel_bias[l], batch=batch, seq=seq),
        functools.partial(_attn_b_prompt, batch=batch, seq=seq),
        g_final, in_rows=1024, mix_rows=512, ffn_rows=1024)
    keep = min(BAND, seq)
    outs_p = (ka[:, -keep:], va[:, -keep:], kb, vb)

    return (yp, ys) + tuple(t[None] for t in outs_p) + tuple(t[None] for t in outs_s)
```

```python
import functools

import jax
import jax.numpy as jnp
from jax import lax
from jax.experimental import pallas as pl
from jax.experimental.pallas import tpu as pltpu

F32 = jnp.float32
BF16 = jnp.bfloat16

HEAD_DIM = 128
CHUNK = 64
BAND_CHUNKS = 8
BAND = BAND_CHUNKS * CHUNK
REL_CLIP = 128
EPS = 1e-6
ATTN_SCALE = HEAD_DIM ** -0.5
LOG2_E = 1.4426950408889634
MASK_VALUE = -1e30

V7X_VMEM_BYTES = 64 * 1024 * 1024
V7X_LANES = 128
V7X_BF16_SUBLANES = 16

EXP_ZERO_BELOW = -104.0

NORM_ROWS = 16
NORM_UNROLL = 8


def _norm_unroll(n_rows):
    steps = n_rows // NORM_ROWS
    return NORM_UNROLL if steps % NORM_UNROLL == 0 else 1


def _vmem_limit(estimate_bytes):
    return int(min(V7X_VMEM_BYTES - (4 << 20), max(estimate_bytes * 5 // 4, 16 << 20)))


def _dot(a, b):
    return jnp.dot(a, b, preferred_element_type=F32)


def _dot_nt(a, b):
    return lax.dot_general(a, b, (((1,), (1,)), ((), ())), preferred_element_type=F32)


def _sigmoid(x):
    return 1.0 / (1.0 + jnp.exp(-x))


def _ada_kernel(c_ref, w_ref, b_ref, o_ref):
    c = c_ref[...]
    a = (c * _sigmoid(c)).astype(BF16)
    o_ref[...] = _dot(a, w_ref[...].astype(BF16)) + b_ref[...]


def _ada(c_all, w_ada, b_ada):
    rows, d = c_all.shape
    n = w_ada.shape[1]
    tn = 1024
    est = 2 * d * tn * 4 + d * tn * 2 + 4 * rows * (d + tn) * 4
    return pl.pallas_call(
        _ada_kernel,
        grid=(n // tn,),
        in_specs=[
            pl.BlockSpec((rows, d), lambda j: (0, 0)),
            pl.BlockSpec((d, tn), lambda j: (0, j)),
            pl.BlockSpec((1, tn), lambda j: (0, j)),
        ],
        out_specs=pl.BlockSpec((rows, tn), lambda j: (0, j)),
        out_shape=jax.ShapeDtypeStruct((rows, n), F32),
        compiler_params=pltpu.CompilerParams(
            dimension_semantics=("parallel",), vmem_limit_bytes=_vmem_limit(est)),
        name="ada",
    )(c_all, w_ada, b_ada.reshape(1, n))


def _norm_mod_rows(x_ref, g_ref, sc_ref, sh_ref, h_ref, groups, ts, copy_ref=None):
    gvec = g_ref[...]
    for gi in range(groups):
        scale = gvec * (1.0 + sc_ref[gi])
        shift = sh_ref[gi]

        def body(r, carry, gi=gi, scale=scale, shift=shift):
            r0 = pl.multiple_of(r * NORM_ROWS, NORM_ROWS)
            x = x_ref[gi, pl.ds(r0, NORM_ROWS), :]
            if copy_ref is not None:
                copy_ref[gi, pl.ds(r0, NORM_ROWS), :] = x
            ms = jnp.mean(x * x, axis=-1, keepdims=True)
            h = x * lax.rsqrt(ms + EPS) * scale + shift
            h_ref[pl.ds(pl.multiple_of(gi * ts + r0, NORM_ROWS), NORM_ROWS), :] = h.astype(BF16)
            return carry

        lax.fori_loop(0, ts // NORM_ROWS, body, 0, unroll=_norm_unroll(ts))


def _row_tiling(batch, seq, tile_rows):
    if seq >= tile_rows:
        assert seq % tile_rows == 0
        groups, ts, ns = 1, tile_rows, seq // tile_rows
    else:
        assert tile_rows % seq == 0 and batch % (tile_rows // seq) == 0
        groups, ts, ns = tile_rows // seq, seq, 1
    n_tiles = batch * seq // (groups * ts)
    if groups == 1:
        x_map = lambda i: (i // ns, i % ns, 0)
        b_map = lambda i: (i // ns, 0, 0)
    else:
        x_map = lambda i: (i, 0, 0)
        b_map = lambda i: (i, 0, 0)
    return groups, ts, n_tiles, x_map, b_map


N_QKV = 6
N_HEADS = 8


N_STAGE = 2


def _inproj_kernel(x_ref, g_ref, sc_ref, sh_ref, w_in_ref,
                   qa_hbm, qb_hbm, ka_hbm, va_hbm, kb_hbm, vb_hbm,
                   ka4_hbm, va4_hbm, kb4_hbm, vb4_hbm, gates_hbm, *rest, groups, ts, cast_w):
    if cast_w:
        w_ref, h_ref, stage16, stage32, sem16, sem32 = rest
        w_ref[...] = w_in_ref[...].astype(BF16)
    else:
        w_ref = w_in_ref
        h_ref, stage16, stage32, sem16, sem32 = rest
    i = pl.program_id(0)
    j = pl.program_id(1)
    tm = groups * ts
    width = w_ref.shape[1]
    rows = pl.ds(pl.multiple_of(i * tm, tm), tm)
    rows4 = pl.ds(pl.multiple_of(i * (tm * N_HEADS), tm * N_HEADS), tm * N_HEADS)
    slot16 = j % N_STAGE
    kv_slot = {1: 0, 2: 1, 4: 0, 5: 1}

    def copy16(slot, dst):
        return pltpu.make_async_copy(stage16.at[slot], dst, sem16.at[slot])

    def copy32(slot, dst_hbm):
        return pltpu.make_async_copy(stage32.at[slot], dst_hbm.at[rows4, :], sem32.at[slot])

    def free16():
        @pl.when(jnp.logical_or(i > 0, j >= N_STAGE))
        def _():
            copy16(slot16, qa_hbm.at[rows, :]).wait()

    @pl.when(j == 0)
    def _():
        _norm_mod_rows(x_ref, g_ref, sc_ref, sh_ref, h_ref, groups, ts)

    for idx, q_hbm, scale in ((0, qa_hbm, ATTN_SCALE * LOG2_E), (3, qb_hbm, ATTN_SCALE)):
        @pl.when(j == idx)
        def _(q_hbm=q_hbm, scale=scale):
            free16()
            stage16[slot16] = (_dot(h_ref[...], w_ref[...]) * scale).astype(BF16)
            copy16(slot16, q_hbm.at[rows, :]).start()

    for idx, kv_hbm, kv4_hbm in ((1, ka_hbm, ka4_hbm), (2, va_hbm, va4_hbm),
                                 (4, kb_hbm, kb4_hbm), (5, vb_hbm, vb4_hbm)):
        @pl.when(j == idx)
        def _(idx=idx, kv_hbm=kv_hbm, kv4_hbm=kv4_hbm):
            slot32 = kv_slot[idx]
            free16()
            if idx in (4, 5):
                copy32(slot32, kv4_hbm).wait()
            else:
                @pl.when(i > 0)
                def _():
                    copy32(slot32, kv4_hbm).wait()
            p = _dot(h_ref[...], w_ref[...])
            stage16[slot16] = p.astype(BF16)
            copy16(slot16, kv_hbm.at[rows, :]).start()
            for hd in range(N_HEADS):
                stage32[slot32, pl.ds(hd, tm, stride=N_HEADS), :] = p[:, hd * HEAD_DIM:(hd + 1) * HEAD_DIM]
            copy32(slot32, kv4_hbm).start()

    @pl.when(j >= N_QKV)
    def _():
        free16()
        stage16[slot16] = _dot(h_ref[...], w_ref[...]).astype(BF16)
        col0 = pl.multiple_of((j - N_QKV) * width, width)
        copy16(slot16, gates_hbm.at[rows, pl.ds(col0, width)]).start()

        @pl.when(jnp.logical_and(i == pl.num_programs(0) - 1, j == pl.num_programs(1) - 1))
        def _():
            for slot in range(N_STAGE):
                copy16(slot, qa_hbm.at[rows, :]).wait()
                copy32(slot, ka4_hbm).wait()


def _inproj(x, g, sc, sh, w_in, tile_rows, width, cast_w=False):
    batch, seq, d = x.shape
    rows = batch * seq
    n_cols = w_in.shape[1]
    assert width == N_HEADS * HEAD_DIM and n_cols % width == 0
    nj = n_cols // width
    n_gate_tiles = nj - N_QKV
    assert nj % N_STAGE == 0 and n_gate_tiles >= N_STAGE
    groups, ts, n_tiles, x_map, b_map = _row_tiling(batch, seq, tile_rows)
    tm = groups * ts
    xm = lambda i, j: x_map(i)
    bm = lambda i, j: b_map(i)
    anywhere = pl.BlockSpec(memory_space=pl.ANY)
    w_spec = pl.BlockSpec((d, width), lambda i, j: (0, j))
    est = (2 * tm * d * 4 + 2 * d * width * w_in.dtype.itemsize + tm * d * 2
           + N_STAGE * tm * width * (2 + 4) + 2 * tm * width * 4)
    bf16_rows = jax.ShapeDtypeStruct((rows, width), BF16)
    f32_flat = jax.ShapeDtypeStruct((rows * N_HEADS, HEAD_DIM), F32)
    cast_specs, cast_shapes = [], []
    if cast_w:
        assert n_tiles == 1 and w_in.dtype == F32
        cast_specs, cast_shapes = [w_spec], [jax.ShapeDtypeStruct(w_in.shape, BF16)]
        est += 2 * d * width * 2
    return pl.pallas_call(
        functools.partial(_inproj_kernel, groups=groups, ts=ts, cast_w=cast_w),
        grid=(n_tiles, nj),
        in_specs=[
            pl.BlockSpec((groups, ts, d), xm),
            pl.BlockSpec((1, d), lambda i, j: (0, 0)),
            pl.BlockSpec((groups, 1, d), bm),
            pl.BlockSpec((groups, 1, d), bm),
            w_spec,
        ],
        out_specs=[anywhere] * 11 + cast_specs,
        out_shape=[bf16_rows] * 6 + [f32_flat] * 4
        + [jax.ShapeDtypeStruct((rows, n_gate_tiles * width), BF16)] + cast_shapes,
        scratch_shapes=[pltpu.VMEM((tm, d), BF16),
                        pltpu.VMEM((N_STAGE, tm, width), BF16),
                        pltpu.VMEM((N_STAGE, tm * N_HEADS, HEAD_DIM), F32),
                        pltpu.SemaphoreType.DMA((N_STAGE,)), pltpu.SemaphoreType.DMA((N_STAGE,))],
        compiler_params=pltpu.CompilerParams(
            dimension_semantics=("arbitrary", "arbitrary"), vmem_limit_bytes=_vmem_limit(est)),
        name="inproj",
    )(x, g.reshape(1, d), sc, sh, w_in)


A_BLOCK = 256


A_HEADS = 8
A_VARIANTS = 3
TOEPLITZ_ROW = 1024


def _toeplitz_row(rel_bias, offset, n_cols):
    m = jnp.arange(TOEPLITZ_ROW)
    m = jnp.where(m < n_cols, m, m - TOEPLITZ_ROW)
    idx = jnp.clip(offset - m, -REL_CLIP, REL_CLIP) + REL_CLIP
    return rel_bias.astype(F32)[:, None, idx]


def _toeplitz_bias(row, n_rows):
    return pltpu.roll(jnp.broadcast_to(row, (n_rows, TOEPLITZ_ROW)), 0, 1, stride=1, stride_axis=0)


def _band_heads(qi, g_ref, q_ref, k_refs, v_refs, o_ref, bias_ref):
    n_keys = 3 * A_BLOCK

    @pl.when(qi == 0)
    def _():
        qc = lax.broadcasted_iota(jnp.int32, (A_BLOCK, n_keys), 0) // CHUNK
        col = lax.broadcasted_iota(jnp.int32, (A_BLOCK, n_keys), 1)
        kc = col // CHUNK - BAND // CHUNK
        in_band = jnp.logical_and(kc <= qc, kc >= qc - BAND_CHUNKS)
        for hh in range(A_HEADS):
            rel = _toeplitz_bias(g_ref[hh], A_BLOCK)[:, :n_keys] * LOG2_E
            full = jnp.where(in_band, rel, MASK_VALUE)
            for v in range(A_VARIANTS):
                bias_ref[v, hh] = jnp.where(col >= (A_VARIANTS - 1 - v) * A_BLOCK, full, MASK_VALUE)

    variant = jnp.minimum(qi, A_VARIANTS - 1)
    for hh in range(A_HEADS):
        cols = slice(hh * HEAD_DIM, (hh + 1) * HEAD_DIM)
        q = q_ref[:, cols]
        scores = [
            _dot_nt(q, k_ref[:, cols].astype(BF16))
            + bias_ref[variant, hh, :, jb * A_BLOCK:(jb + 1) * A_BLOCK]
            for jb, k_ref in enumerate(k_refs)]
        m = jnp.max(functools.reduce(jnp.maximum, scores), axis=-1, keepdims=True)
        es = [jnp.exp2(s - m) for s in scores]
        l = jnp.sum(functools.reduce(jnp.add, es), axis=-1, keepdims=True)
        acc = functools.reduce(jnp.add, [
            _dot(e.astype(BF16), v_ref[:, cols].astype(BF16))
            for e, v_ref in zip(es, v_refs)])
        o_ref[:, cols] = (acc / l).astype(o_ref.dtype)


def _head_rows(ref, first_pos, n_pos, head):
    return ref[pl.ds(first_pos * N_HEADS + head, n_pos, stride=N_HEADS), :]


def _attn_a_sample_kernel(g_ref, q_ref, kc_ref, vc_ref, kn_ref, vn_ref, o_ref, *, lc):
    n = q_ref.shape[0]
    kc_ref = kc_ref.at[0]
    vc_ref = vc_ref.at[0]
    for hh in range(N_HEADS):
        cols = slice(hh * HEAD_DIM, (hh + 1) * HEAD_DIM)
        q = q_ref[:, cols]
        bias = _toeplitz_bias(g_ref[hh], n) * LOG2_E
        sc = _dot_nt(q, _head_rows(kc_ref, 0, lc, hh).astype(BF16)) + bias[:, :lc]
        sn = _dot_nt(q, kn_ref[:, cols]) + bias[:, lc:lc + n]
        m = jnp.maximum(jnp.max(sc, axis=-1, keepdims=True), jnp.max(sn, axis=-1, keepdims=True))
        ec = jnp.exp2(sc - m)
        en = jnp.exp2(sn - m)
        l = jnp.sum(ec, axis=-1, keepdims=True) + jnp.sum(en, axis=-1, keepdims=True)
        acc = (_dot(ec.astype(BF16), _head_rows(vc_ref, 0, lc, hh).astype(BF16))
               + _dot(en.astype(BF16), vn_ref[:, cols]))
        o_ref[:, cols] = (acc / l).astype(o_ref.dtype)


def _attn_a_sample(q, k, v, cache_k, cache_v, rel_bias, batch, n, past_len):
    rows, width = q.shape
    lc = cache_k.shape[1] // N_HEADS
    assert past_len >= lc and lc + 2 * n - 1 <= TOEPLITZ_ROW and lc % V7X_LANES == 0
    assert width == N_HEADS * HEAD_DIM
    g = _toeplitz_row(rel_bias, lc, lc + n)
    blk = pl.BlockSpec((n, width), lambda b: (b, 0))
    cblk = pl.BlockSpec((1, lc * N_HEADS, HEAD_DIM), lambda b: (b, 0, 0))
    est = 2 * 2 * lc * width * 4 + 16 * n * lc * 4 * N_HEADS
    return pl.pallas_call(
        functools.partial(_attn_a_sample_kernel, lc=lc),
        grid=(batch,),
        in_specs=[pl.BlockSpec((N_HEADS, 1, TOEPLITZ_ROW), lambda b: (0, 0, 0)),
                  blk, cblk, cblk, blk, blk],
        out_specs=blk,
        out_shape=jax.ShapeDtypeStruct((rows, width), BF16),
        compiler_params=pltpu.CompilerParams(
            dimension_semantics=("parallel",), vmem_limit_bytes=_vmem_limit(est)),
        name="attn_a_sample",
    )(g, q, cache_k, cache_v, k, v)


B_BLOCK = 256


def _later_matrix(n):
    j = jnp.arange(n)[:, None]
    s = jnp.arange(n)[None, :]
    return (j > s).astype(BF16)


def _strict_causal_mask(n):
    t = jnp.arange(n)[:, None]
    s = jnp.arange(n)[None, :]
    return jnp.where(s < t, 0.0, MASK_VALUE).astype(F32)


def _stick_terms(q, k, later_mat, z_mask=None, valid=None):
    z = _dot_nt(q, k)
    if z_mask is not None:
        z = z + z_mask
    if valid is not None:
        z = jnp.where(valid, z, MASK_VALUE)
    log_1m = -(jnp.maximum(z, 0.0) + jnp.log(1.0 + jnp.exp(-jnp.abs(z))))
    hi = log_1m.astype(BF16)
    lo = (log_1m - hi.astype(F32)).astype(BF16)
    later = _dot(hi, later_mat) + _dot(lo, later_mat)
    return z + log_1m + later, jnp.sum(log_1m, axis=-1, keepdims=True)


def _stick_sweep_past(qs, load_block, n_blocks, cs, accs, later_mat):
    heads = len(qs)

    def cond(carry):
        j, cs, _ = carry
        return jnp.logical_and(j >= 0, jnp.max(functools.reduce(jnp.maximum, cs)) > EXP_ZERO_BELOW)

    def body(carry):
        j, cs, accs = carry
        k_of, v_of = load_block(j)
        new_cs, new_accs = [], []
        for hh in range(heads):
            logit, row_sum = _stick_terms(qs[hh], k_of(hh).astype(BF16), later_mat)
            weights = jnp.exp(logit + cs[hh]).astype(BF16)
            new_accs.append(accs[hh] + _dot(weights, v_of(hh).astype(BF16)))
            new_cs.append(cs[hh] + row_sum)
        return j - 1, tuple(new_cs), tuple(new_accs)

    _, _, accs = lax.while_loop(cond, body, (n_blocks - 1, tuple(cs), tuple(accs)))
    return accs


def _stick_heads(q_ref, own, prev, later_own, mask_own, later_mat, prev_valid, load_block, n_blocks,
                 o_ref):
    heads = q_ref.shape[1] // HEAD_DIM
    qs, cs, accs = [], [], []
    for hh in range(heads):
        q = q_ref[:, hh * HEAD_DIM:(hh + 1) * HEAD_DIM]
        logit0, sum0 = _stick_terms(q, own[0](hh).astype(BF16), later_own, z_mask=mask_own)
        logit1, sum1 = _stick_terms(q, prev[0](hh).astype(BF16), later_mat, valid=prev_valid)
        accs.append(_dot(jnp.exp(logit0).astype(BF16), own[1](hh).astype(BF16))
                    + _dot(jnp.exp(logit1 + sum0).astype(BF16), prev[1](hh).astype(BF16)))
        cs.append(sum0 + sum1)
        qs.append(q)
    accs = _stick_sweep_past(qs, load_block, n_blocks, cs, accs, later_mat)
    for hh in range(heads):
        o_ref[:, hh * HEAD_DIM:(hh + 1) * HEAD_DIM] = accs[hh].astype(o_ref.dtype)


def _lane_heads(k_ref, v_ref, rows):
    cols = lambda hh: slice(hh * HEAD_DIM, (hh + 1) * HEAD_DIM)
    return (lambda hh: k_ref[rows, cols(hh)]), (lambda hh: v_ref[rows, cols(hh)])


def _row_heads(k_ref, v_ref, n_pos):
    return (lambda hh: _head_rows(k_ref, 0, n_pos, hh)), (lambda hh: _head_rows(v_ref, 0, n_pos, hh))


def _attn_prompt_kernel(g_ref, qa_ref, ka0_ref, ka1_ref, ka2_ref, va0_ref, va1_ref, va2_ref,
                        qb_ref, kb0_ref, vb0_ref, kb1_ref, vb1_ref, kb_hbm, vb_hbm, u_ref, mask_ref,
                        oa_ref, ob_ref, bias_ref, kbuf, vbuf, sems, *, nq):
    b = pl.program_id(0)
    qi = pl.program_id(1)
    _band_heads(qi, g_ref, qa_ref, (ka0_ref, ka1_ref, ka2_ref), (va0_ref, va1_ref, va2_ref), oa_ref,
                bias_ref)

    def fetch_block(j):
        rows = pl.ds(pl.multiple_of((b * nq + j) * B_BLOCK, B_BLOCK), B_BLOCK)
        copies = (pltpu.make_async_copy(kb_hbm.at[rows, :], kbuf, sems.at[0]),
                  pltpu.make_async_copy(vb_hbm.at[rows, :], vbuf, sems.at[1]))
        for cp in copies:
            cp.start()
        for cp in copies:
            cp.wait()
        return _lane_heads(kbuf, vbuf, slice(None))

    later_mat = u_ref[...]
    _stick_heads(qb_ref, _lane_heads(kb0_ref, vb0_ref, slice(None)),
                 _lane_heads(kb1_ref, vb1_ref, slice(None)), later_mat, mask_ref[...], later_mat,
                 qi > 0, fetch_block, qi - 1, ob_ref)


def _attn_prompt(qa, ka, va, qb, kb, vb, rel_bias, batch, seq):
    rows, width = qa.shape
    nq = seq // A_BLOCK
    assert A_BLOCK == B_BLOCK and BAND == 2 * A_BLOCK and seq % A_BLOCK == 0
    assert width == A_HEADS * HEAD_DIM and 4 * A_BLOCK - 1 <= TOEPLITZ_ROW
    g = _toeplitz_row(rel_bias, BAND, 3 * A_BLOCK)
    shape = (A_BLOCK, width)
    blk = lambda back: pl.BlockSpec(shape, lambda b, i: (b * nq + jnp.maximum(i - back, 0), 0))
    const = pl.BlockSpec((B_BLOCK, B_BLOCK), lambda b, i: (0, 0))
    anywhere = pl.BlockSpec(memory_space=pl.ANY)
    bias_bytes = A_VARIANTS * A_HEADS * A_BLOCK * 3 * A_BLOCK * 4
    est = 2 * 14 * A_BLOCK * width * 2 + bias_bytes + A_HEADS * 14 * A_BLOCK * 3 * A_BLOCK * 4
    out = jax.ShapeDtypeStruct((rows, width), BF16)
    return pl.pallas_call(
        functools.partial(_attn_prompt_kernel, nq=nq),
        grid=(batch, nq),
        in_specs=[pl.BlockSpec((A_HEADS, 1, TOEPLITZ_ROW), lambda b, i: (0, 0, 0)),
                  blk(0), blk(2), blk(1), blk(0), blk(2), blk(1), blk(0),
                  blk(0), blk(0), blk(0), blk(1), blk(1), anywhere, anywhere, const, const],
        out_specs=[blk(0), blk(0)],
        out_shape=[out, out],
        scratch_shapes=[pltpu.VMEM((A_VARIANTS, A_HEADS, A_BLOCK, 3 * A_BLOCK), F32),
                        pltpu.VMEM(shape, BF16), pltpu.VMEM(shape, BF16),
                        pltpu.SemaphoreType.DMA((2,))],
        compiler_params=pltpu.CompilerParams(
            dimension_semantics=("parallel", "arbitrary"), vmem_limit_bytes=_vmem_limit(est)),
        name="attn_prompt",
    )(g, qa, ka, ka, ka, va, va, va, qb, kb, vb, kb, vb, kb, vb,
      _later_matrix(B_BLOCK), _strict_causal_mask(B_BLOCK))


def _attn_b_sample_kernel(q_ref, kn_ref, vn_ref, k1_ref, v1_ref, kc_hbm, vc_hbm, un_ref, mask_ref, u_ref,
                          o_ref, kbuf, vbuf, sems, *, n_blocks):
    b = pl.program_id(0)
    block_rows = B_BLOCK * N_HEADS

    def fetch_block(j):
        rows = pl.ds(pl.multiple_of(j * block_rows, block_rows), block_rows)
        copies = (pltpu.make_async_copy(kc_hbm.at[b, rows, :], kbuf, sems.at[0]),
                  pltpu.make_async_copy(vc_hbm.at[b, rows, :], vbuf, sems.at[1]))
        for cp in copies:
            cp.start()
        for cp in copies:
            cp.wait()
        return _row_heads(kbuf, vbuf, B_BLOCK)

    _stick_heads(q_ref, _lane_heads(kn_ref, vn_ref, slice(None)),
                 _row_heads(k1_ref.at[0], v1_ref.at[0], B_BLOCK),
                 un_ref[...], mask_ref[...], u_ref[...], None, fetch_block, n_blocks - 1, o_ref)


def _attn_b_sample(q, k, v, cache_k, cache_v, batch, n):
    rows, width = q.shape
    lc = cache_k.shape[1] // N_HEADS
    assert lc % B_BLOCK == 0 and lc >= B_BLOCK and width == N_HEADS * HEAD_DIM
    n_blocks = lc // B_BLOCK
    block_rows = B_BLOCK * N_HEADS
    blk = pl.BlockSpec((n, width), lambda b: (b, 0))
    newest = pl.BlockSpec((1, block_rows, HEAD_DIM), lambda b: (b, n_blocks - 1, 0))
    anywhere = pl.BlockSpec(memory_space=pl.ANY)
    const = lambda shape: pl.BlockSpec(shape, lambda b: (0, 0))
    est = 6 * block_rows * HEAD_DIM * 4 + N_HEADS * 40 * n * B_BLOCK * 4 + 8 * B_BLOCK * B_BLOCK
    return pl.pallas_call(
        functools.partial(_attn_b_sample_kernel, n_blocks=n_blocks),
        grid=(batch,),
        in_specs=[blk, blk, blk, newest, newest, anywhere, anywhere,
                  const((n, n)), const((n, n)), const((B_BLOCK, B_BLOCK))],
        out_specs=blk,
        out_shape=jax.ShapeDtypeStruct((rows, width), BF16),
        scratch_shapes=[pltpu.VMEM((block_rows, HEAD_DIM), F32), pltpu.VMEM((block_rows, HEAD_DIM), F32),
                        pltpu.SemaphoreType.DMA((2,))],
        compiler_params=pltpu.CompilerParams(
            dimension_semantics=("arbitrary",), vmem_limit_bytes=_vmem_limit(est)),
        name="attn_b_sample",
    )(q, k, v, cache_k, cache_v, cache_k, cache_v,
      _later_matrix(n), _strict_causal_mask(n), _later_matrix(B_BLOCK))


MIX_COLS = 512


def _mix_out_kernel(x_ref, oa_ref, ob_ref, gates_ref, gt_ref, wa_ref, wb_ref, wo_ref, o_ref, y_ref,
                    *, groups, ts):
    d = o_ref.shape[-1]
    oa = oa_ref[...]
    ob = ob_ref[...]
    for c0 in range(0, d, MIX_COLS):
        cols = slice(c0, c0 + MIX_COLS)
        ya = _dot(oa, wa_ref[:, cols])
        yb = _dot(ob, wb_ref[:, cols])
        ga = gates_ref[:, cols].astype(F32)
        gb = gates_ref[:, d + c0:d + c0 + MIX_COLS].astype(F32)
        y_ref[:, cols] = (_sigmoid(ga) * ya + _sigmoid(gb) * yb).astype(BF16)
    y = y_ref[...]
    for c0 in range(0, d, MIX_COLS):
        cols = slice(c0, c0 + MIX_COLS)
        merged = _dot(y, wo_ref[:, cols])
        for gi in range(groups):
            rows = slice(gi * ts, (gi + 1) * ts)
            o_ref[gi, :, cols] = x_ref[gi, :, cols] + gt_ref[gi, :, cols] * merged[rows]


def _mix_out(x, oa, ob, gates, gt, wa, wb, wo, tile_rows):
    batch, seq, d = x.shape
    width = oa.shape[1]
    groups, ts, n_tiles, x_map, b_map = _row_tiling(batch, seq, tile_rows)
    tm = groups * ts
    const = lambda i: (0, 0)
    single = pl.Buffered(1)
    est = (2 * 2 * tm * d * 4 + 2 * 2 * tm * width * 2 + 2 * tm * 2 * d * 2 + tm * d * 2
           + (2 * width * d + d * d) * 2 + 6 * tm * MIX_COLS * 4)
    return pl.pallas_call(
        functools.partial(_mix_out_kernel, groups=groups, ts=ts),
        grid=(n_tiles,),
        in_specs=[
            pl.BlockSpec((groups, ts, d), x_map),
            pl.BlockSpec((tm, width), lambda i: (i, 0)),
            pl.BlockSpec((tm, width), lambda i: (i, 0)),
            pl.BlockSpec((tm, 2 * d), lambda i: (i, 0)),
            pl.BlockSpec((groups, 1, d), b_map),
            pl.BlockSpec((width, d), const, pipeline_mode=single),
            pl.BlockSpec((width, d), const, pipeline_mode=single),
            pl.BlockSpec((d, d), const, pipeline_mode=single),
        ],
        out_specs=pl.BlockSpec((groups, ts, d), x_map),
        out_shape=jax.ShapeDtypeStruct((batch, seq, d), F32),
        scratch_shapes=[pltpu.VMEM((tm, d), BF16)],
        compiler_params=pltpu.CompilerParams(
            dimension_semantics=("parallel",), vmem_limit_bytes=_vmem_limit(est)),
        name="mix_out",
    )(x, oa, ob, gates, gt, wa, wb, wo)


FF_TILE = 512
FF_OUT_COLS = 512


def _ffn_kernel(x_hbm, g_ref, sc_ref, sh_ref, gt_ref, gf_ref, wg_in_ref, wu_in_ref, wd_in_ref, o_ref,
                *rest, groups, ts, tiles_per_batch, cast_w):
    if cast_w:
        wg_ref, wu_ref, wd_ref, x_buf, h_ref, x_sem = rest
        wg_ref[...] = wg_in_ref[...].astype(BF16)
        wu_ref[...] = wu_in_ref[...].astype(BF16)
        wd_ref[...] = wd_in_ref[...].astype(BF16)
    else:
        wg_ref, wu_ref, wd_ref = wg_in_ref, wu_in_ref, wd_in_ref
        x_buf, h_ref, x_sem = rest
    i = pl.program_id(0)
    f = pl.program_id(1)

    def x_copy(tile):
        if groups == 1:
            src = x_hbm.at[pl.ds(tile // tiles_per_batch, 1),
                           pl.ds(pl.multiple_of((tile % tiles_per_batch) * ts, ts), ts), :]
        else:
            src = x_hbm.at[pl.ds(pl.multiple_of(tile * groups, groups), groups), :, :]
        return pltpu.make_async_copy(src, x_buf, x_sem.at[0])

    @pl.when(f == 0)
    def _():
        @pl.when(i == 0)
        def _():
            x_copy(0).start()

        x_copy(i).wait()
        _norm_mod_rows(x_buf, g_ref, sc_ref, sh_ref, h_ref, groups, ts, copy_ref=o_ref)

    @pl.when(jnp.logical_and(f == 1, i + 1 < pl.num_programs(0)))
    def _():
        x_copy(i + 1).start()

    h = h_ref[...]
    gate = _dot(h, wg_ref[...])
    up = _dot(h, wu_ref[...])
    hidden = (gate * _sigmoid(gate) * up).astype(BF16)
    d = o_ref.shape[-1]
    for c0 in range(0, d, FF_OUT_COLS):
        cols = slice(c0, c0 + FF_OUT_COLS)
        part = _dot(hidden, wd_ref[:, cols])
        for gi in range(groups):
            o_ref[gi, :, cols] += gt_ref[gi, :, cols] * part[gi * ts:(gi + 1) * ts]

    @pl.when(f == pl.num_programs(1) - 1)
    def _():
        gf = gf_ref[...]
        for gi in range(groups):
            for r0 in range(0, ts, NORM_ROWS):
                rows = slice(r0, r0 + NORM_ROWS)
                x2 = o_ref[gi, rows, :]
                ms = jnp.mean(x2 * x2, axis=-1, keepdims=True)
                o_ref[gi, rows, :] = x2 * lax.rsqrt(ms + EPS) * gf


def _ffn(x, g, sc, sh, gt, g_final, w_gate, w_up, w_down, tile_rows, cast_w=False):
    batch, seq, d = x.shape
    d_ff = w_down.shape[0]
    nf = d_ff // FF_TILE
    up_offset = nf if w_up.shape[1] == 2 * d_ff else 0
    groups, ts, n_tiles, x_map, b_map = _row_tiling(batch, seq, tile_rows)
    tm = groups * ts
    xm = lambda i, f: x_map(i)
    bm = lambda i, f: b_map(i)
    vec = pl.BlockSpec((1, d), lambda i, f: (0, 0))
    mod = pl.BlockSpec((groups, 1, d), bm)
    assert nf >= 2
    est = (3 * tm * d * 4 + tm * d * 2 + 2 * 3 * d * FF_TILE * w_down.dtype.itemsize
           + 5 * tm * FF_TILE * 4)
    col_tile = pl.BlockSpec((d, FF_TILE), lambda i, f: (0, f))
    row_tile = pl.BlockSpec((FF_TILE, d), lambda i, f: (f, 0))
    cast_specs, cast_shapes = [], []
    if cast_w:
        assert n_tiles == 1 and w_down.dtype == F32 and up_offset == nf
        cast_specs = [col_tile, col_tile, row_tile]
        cast_shapes = [jax.ShapeDtypeStruct((d, d_ff), BF16)] * 2 + [jax.ShapeDtypeStruct((d_ff, d), BF16)]
        est += 2 * 3 * d * FF_TILE * 2
    return pl.pallas_call(
        functools.partial(_ffn_kernel, groups=groups, ts=ts, tiles_per_batch=max(seq // ts, 1),
                          cast_w=cast_w),
        grid=(n_tiles, nf),
        in_specs=[
            pl.BlockSpec(memory_space=pl.ANY), vec, mod, mod, mod, vec,
            col_tile,
            pl.BlockSpec((d, FF_TILE), lambda i, f: (0, up_offset + f)),
            row_tile,
        ],
        out_specs=[pl.BlockSpec((groups, ts, d), xm)] + cast_specs,
        out_shape=[jax.ShapeDtypeStruct((batch, seq, d), F32)] + cast_shapes,
        scratch_shapes=[pltpu.VMEM((groups, ts, d), F32), pltpu.VMEM((tm, d), BF16),
                        pltpu.SemaphoreType.DMA((1,))],
        compiler_params=pltpu.CompilerParams(
            dimension_semantics=("arbitrary", "arbitrary"), vmem_limit_bytes=_vmem_limit(est)),
        name="ffn",
    )(x, g.reshape(1, d), sc, sh, gt, g_final.reshape(1, d), w_gate, w_up, w_down)


def _layer(x, mods, w, attn, g_final, in_rows, mix_rows, ffn_rows, cast_w=False):
    batch, seq, d = x.shape
    sh1, sc1, gt1, sh2, sc2, gt2 = mods
    width = w["w_a_out"].shape[0]
    qa, qb, ka, va, kb, vb, ka4, va4, kb4, vb4, gates, *w_in16 = _inproj(
        x, w["g_mix"], sc1, sh1, w["w_in"], in_rows, width, cast_w)
    oa, ob = attn(qa, ka, va, qb, kb, vb)
    x1 = _mix_out(x, oa, ob, gates, gt1, w["w_a_out"], w["w_b_out"], w["w_o"], mix_rows)
    y, *w_ffn16 = _ffn(x1, w["g_ffn"], sc2, sh2, gt2, g_final, w["w_gate"], w["w_up"], w["w_down"],
                       ffn_rows, cast_w)
    heads_of = lambda t: t.reshape(batch, seq, N_HEADS, HEAD_DIM)
    cast = dict(zip(("w_in", "w_gate", "w_up", "w_down"), w_in16 + w_ffn16))
    return (y, heads_of(ka4), heads_of(va4), heads_of(kb4), heads_of(vb4)), cast


def kernel(x_prompt, x_sample, c_prompt, c_sample, cache_a_k, cache_a_v, cache_b_k, cache_b_v,
           w_ada, b_ada, g_mix, w_in, rel_bias, w_a_out, w_b_out, w_o, g_ffn, w_gate_up, w_down,
           g_final):
    depth = w_in.shape[0]
    assert depth == 1
    batch, seq, d = x_prompt.shape
    dec_batch, dec_seq, _ = x_sample.shape
    past_len = cache_b_k.shape[2]
    l = 0

    n_c = batch + dec_batch
    c_rows = -(-n_c // 8) * 8
    c_all = jnp.concatenate([c_prompt, c_sample, jnp.zeros((c_rows - n_c, d), F32)], axis=0)
    mod_all = _ada(c_all, w_ada[l], b_ada[l])

    def mods_of(rows):
        return tuple(m[:, None, :] for m in jnp.split(rows, 6, axis=-1))

    mods_p = mods_of(mod_all[:batch])
    mods_s = mods_of(mod_all[batch:n_c])

    w_sample = {
        "g_mix": g_mix[l], "g_ffn": g_ffn[l],
        "w_a_out": w_a_out[l].astype(BF16), "w_b_out": w_b_out[l].astype(BF16),
        "w_o": w_o[l].astype(BF16),
        "w_in": w_in[l], "w_gate": w_gate_up[l], "w_up": w_gate_up[l], "w_down": w_down[l],
    }

    flat_rows = lambda c: c[l].reshape(dec_batch, c.shape[2] * c.shape[3], c.shape[4])
    cak, cav, cbk, cbv = (flat_rows(c) for c in (cache_a_k, cache_a_v, cache_b_k, cache_b_v))
    def attn_sample(qa, ka, va, qb, kb, vb):
        return (_attn_a_sample(qa, ka, va, cak, cav, rel_bias[l], dec_batch, dec_seq, past_len),
                _attn_b_sample(qb, kb, vb, cbk, cbv, dec_batch, dec_seq))

    (ys, *outs_s), cast = _layer(
        x_sample, mods_s, w_sample, attn_sample,
        g_final, in_rows=dec_batch * dec_seq, mix_rows=dec_batch * dec_seq,
        ffn_rows=dec_batch * dec_seq, cast_w=True)

    (yp, ka, va, kb, vb), _ = _layer(
        x_prompt, mods_p, {**w_sample, **cast},
        functools.partial(_attn_prompt, rel_bias=rel_bias[l], batch=batch, seq=seq),
        g_final, in_rows=1024, mix_rows=512, ffn_rows=1024)
    keep = min(BAND, seq)
    outs_p = (ka[:, -keep:], va[:, -keep:], kb, vb)

    return (yp, ys) + tuple(t[None] for t in outs_p) + tuple(t[None] for t in outs_s)
```
